```python
import jax
import jax.numpy as jnp
from jax import lax
import numpy as np

D_MODEL = 2048
BATCH = 1
SEQ = 8192
DEPTH = 2
DEC_BATCH = 128
DEC_SEQ = 8
PAST_LEN = 2048
PAGE_SIZE = 128

N_HEADS = 16
HEAD_DIM = D_MODEL // N_HEADS
N_KV_HEADS = 4
GQA_GROUP = N_HEADS // N_KV_HEADS
QKV_DIM = (N_HEADS + 2 * N_KV_HEADS) * HEAD_DIM
SCALE = HEAD_DIM ** -0.5
Q_BLOCK = 128
N_EXPERTS = 16
N_GROUPS = 4
EXPERTS_PER_GROUP = N_EXPERTS // N_GROUPS
TOP_K = 2
GROUP_SCORE_TOPK = 2
D_EXPERT = 768
LN_EPS = 1e-5
DEEPNORM_ALPHA = (2.0 * DEPTH) ** 0.25
DEEPNORM_BETA = (8.0 * DEPTH) ** -0.25
N_SB_LAYERS = (DEPTH + 1) // 2
N_FOX_LAYERS = DEPTH // 2
FORGET_BIAS_INIT = 2.0
ADA_INIT = 0.5

kernel_name = 'sb_fox_hybrid_grouped_moe_step'


def _layernorm(x, g, b):
    xf = x.astype(jnp.float32)
    mu = jnp.mean(xf, axis=-1, keepdims=True)
    var = jnp.mean(jnp.square(xf - mu), axis=-1, keepdims=True)
    return ((xf - mu) * lax.rsqrt(var + LN_EPS) * g + b).astype(x.dtype)


def _gather_pages(pool, page_table):
    g = pool[page_table]
    return g.reshape(g.shape[0], g.shape[1] * g.shape[2], *g.shape[3:])


def _num_blocks(t):
    return t // Q_BLOCK if t % Q_BLOCK == 0 else 1


def _to_blocks(a, nb):
    return jnp.moveaxis(a.reshape(a.shape[0], nb, a.shape[1] // nb, *a.shape[2:]), 1, 0)


def _from_blocks(a):
    a = jnp.moveaxis(a, 0, 1)
    return a.reshape(a.shape[0], a.shape[1] * a.shape[2], *a.shape[3:])


def _split_qkv(proj):
    b, t = proj.shape[:2]
    nq, nkv = N_HEADS * HEAD_DIM, N_KV_HEADS * HEAD_DIM
    q = proj[..., :nq].reshape(b, t, N_KV_HEADS, GQA_GROUP, HEAD_DIM)
    k = proj[..., nq:nq + nkv].reshape(b, t, N_KV_HEADS, HEAD_DIM)
    v = proj[..., nq + nkv:nq + 2 * nkv].reshape(b, t, N_KV_HEADS, HEAD_DIM)
    return q, k, v


def _stick_breaking_attention(q, k, v, q_pos, k_pos):
    nb = _num_blocks(q.shape[1])

    def block(args):
        qb, pb = args
        z = jnp.einsum('bqkgd,bskd->bkgqs', qb, k, preferred_element_type=jnp.float32) * SCALE
        valid = k_pos[None, :] < pb[:, None]
        log_1m_beta = jnp.where(valid, jax.nn.log_sigmoid(-z), 0.0)
        between = lax.cumsum(log_1m_beta, axis=z.ndim - 1, reverse=True) - log_1m_beta
        w = jnp.where(valid, jnp.exp(jax.nn.log_sigmoid(z) + between), 0.0)
        return jnp.einsum('bkgqs,bskd->bqkgd', w.astype(v.dtype), v)

    out = lax.map(block, (_to_blocks(q, nb), q_pos.reshape(nb, -1)))
    return _from_blocks(out)


def _forgetting_attention(q, k, v, cum_q, cum_k, q_pos, k_pos):
    nb = _num_blocks(q.shape[1])
    b, s = cum_k.shape[:2]
    ck = jnp.moveaxis(cum_k.reshape(b, s, N_KV_HEADS, GQA_GROUP), 1, 3)

    def block(args):
        qb, cqb, pb = args
        cq = jnp.moveaxis(cqb.reshape(cqb.shape[0], cqb.shape[1], N_KV_HEADS, GQA_GROUP), 1, 3)
        z = (jnp.einsum('bqkgd,bskd->bkgqs', qb, k, preferred_element_type=jnp.float32) * SCALE
             + cq[..., :, None] - ck[..., None, :])
        valid = k_pos[None, :] <= pb[:, None]
        p = jax.nn.softmax(jnp.where(valid, z, -jnp.inf), axis=-1)
        return jnp.einsum('bkgqs,bskd->bqkgd', p.astype(v.dtype), v)

    out = lax.map(block, (_to_blocks(q, nb), _to_blocks(cum_q, nb), q_pos.reshape(nb, -1)))
    return _from_blocks(out)


def _sb_mixer(h, w_in, w_out, k_past, v_past):
    b, t, _ = h.shape
    q, k, v = _split_qkv(h @ w_in)
    if k_past is None:
        p, k_all, v_all = 0, k, v
    else:
        p = k_past.shape[1]
        k_all = jnp.concatenate([k_past.astype(k.dtype), k], axis=1)
        v_all = jnp.concatenate([v_past.astype(v.dtype), v], axis=1)
    q_pos = p + jnp.arange(t, dtype=jnp.int32)
    k_pos = jnp.arange(p + t, dtype=jnp.int32)
    o = _stick_breaking_attention(q, k_all, v_all, q_pos, k_pos)
    return o.reshape(b, t, D_MODEL) @ w_out, k, v


def _fox_mixer(h, w_in, b_f, w_out, k_past, v_past, logf_past):
    b, t, _ = h.shape
    proj = h @ w_in
    q, k, v = _split_qkv(proj)
    logf = jax.nn.log_sigmoid((proj[..., QKV_DIM:] + b_f).astype(jnp.float32))
    if k_past is None:
        p, k_all, v_all, logf_all = 0, k, v, logf
    else:
        p = k_past.shape[1]
        k_all = jnp.concatenate([k_past.astype(k.dtype), k], axis=1)
        v_all = jnp.concatenate([v_past.astype(v.dtype), v], axis=1)
        logf_all = jnp.concatenate([logf_past.astype(jnp.float32), logf], axis=1)
    cum = lax.cumsum(logf_all, axis=1)
    q_pos = p + jnp.arange(t, dtype=jnp.int32)
    k_pos = jnp.arange(p + t, dtype=jnp.int32)
    o = _forgetting_attention(q, k_all, v_all, cum[:, p:], cum, q_pos, k_pos)
    return o.reshape(b, t, D_MODEL) @ w_out, k, v, logf


def _grouped_moe(h, router_w, router_b, w_gate, w_up, w_down):
    b, t, d = h.shape
    tok = h.reshape(b * t, d)
    probs = jax.nn.softmax((tok @ router_w).astype(jnp.float32), axis=-1)
    sel = (probs + router_b.astype(jnp.float32)).reshape(-1, N_GROUPS, EXPERTS_PER_GROUP)
    group_score = jnp.sum(lax.top_k(sel, GROUP_SCORE_TOPK)[0], axis=-1)
    grp = jnp.argmax(group_score, axis=-1)
    in_group = jnp.take_along_axis(sel, grp[:, None, None], axis=1)[:, 0]
    _, idx = lax.top_k(in_group, TOP_K)
    eid = grp[:, None] * EXPERTS_PER_GROUP + idx
    gate = jnp.take_along_axis(probs, eid, axis=-1)
    gate = gate / jnp.sum(gate, axis=-1, keepdims=True)
    combine = jnp.einsum('nk,nke->ne', gate, jax.nn.one_hot(eid, N_EXPERTS, dtype=jnp.float32))
    act = jax.nn.silu(jnp.einsum('nd,edf->nef', tok, w_gate)) * jnp.einsum('nd,edf->nef', tok, w_up)
    y = jnp.einsum('nef,efd->nd', act * combine[:, :, None].astype(act.dtype), w_down)
    return y.reshape(b, t, d)


def _run_trunk(x, c, page_table, cache_k_sb, cache_v_sb, cache_k_fox, cache_v_fox, cache_logf_fox,
               ada_w, ada_b, ln_g, ln_b, w_in_sb, w_out_sb, w_in_fox, b_f_fox, w_out_fox,
               router_w, router_b, w_gate, w_up, w_down):
    paged = page_table is not None
    sb_new, fox_new = [], []
    for i in range(DEPTH):
        j = i // 2
        mod = jax.nn.silu(c) @ ada_w[i] + ada_b[i]
        sh1, sc1, g1, sh2, sc2, g2 = jnp.split(mod[:, None, :], 6, axis=-1)
        h = x * (1 + sc1) + sh1
        if i % 2 == 0:
            k_past = _gather_pages(cache_k_sb[j], page_table) if paged else None
            v_past = _gather_pages(cache_v_sb[j], page_table) if paged else None
            y, k, v = _sb_mixer(h, w_in_sb[j], w_out_sb[j], k_past, v_past)
            sb_new.append((k, v))
        else:
            k_past = _gather_pages(cache_k_fox[j], page_table) if paged else None
            v_past = _gather_pages(cache_v_fox[j], page_table) if paged else None
            lf_past = _gather_pages(cache_logf_fox[j], page_table) if paged else None
            y, k, v, lf = _fox_mixer(h, w_in_fox[j], b_f_fox[j], w_out_fox[j], k_past, v_past, lf_past)
            fox_new.append((k, v, lf))
        x = _layernorm(DEEPNORM_ALPHA * x + g1 * y, ln_g[i, 0], ln_b[i, 0])
        h = x * (1 + sc2) + sh2
        y = _grouped_moe(h, router_w, router_b, w_gate[i], w_up[i], w_down[i])
        x = _layernorm(DEEPNORM_ALPHA * x + g2 * y, ln_g[i, 1], ln_b[i, 1])
    return x, sb_new, fox_new


def setup_inputs(seed: int = 0) -> dict:
    key = jax.random.key(seed)
    ks = jax.random.split(key, 24)
    D = D_MODEL
    n_pages = PAST_LEN // PAGE_SIZE
    n_used = DEC_BATCH * n_pages
    n_pool = n_used + max(1, n_used // 4)

    def nrm(k, shape, s):
        return jax.random.normal(k, shape, jnp.float32) * s

    v_lo = (N_HEADS + N_KV_HEADS) * HEAD_DIM
    v_hi = QKV_DIM
    page_table = jax.random.permutation(ks[0], n_pool)[:n_used].reshape(DEC_BATCH, n_pages).astype(jnp.int32)
    sb_shape = (N_SB_LAYERS, n_pool, PAGE_SIZE, N_KV_HEADS, HEAD_DIM)
    fox_shape = (N_FOX_LAYERS, n_pool, PAGE_SIZE, N_KV_HEADS, HEAD_DIM)
    w_in_sb = nrm(ks[10], (N_SB_LAYERS, D, QKV_DIM), D ** -0.5)
    w_in_sb = w_in_sb.at[..., v_lo:v_hi].multiply(DEEPNORM_BETA)
    w_in_fox = nrm(ks[12], (N_FOX_LAYERS, D, QKV_DIM + N_HEADS), D ** -0.5)
    w_in_fox = w_in_fox.at[..., v_lo:v_hi].multiply(DEEPNORM_BETA)
    return {
        'x_prompt': nrm(ks[1], (BATCH, SEQ, D), 1.0),
        'x_sample': nrm(ks[2], (DEC_BATCH, DEC_SEQ, D), 1.0),
        'cache_k_sb': nrm(ks[3], sb_shape, 1.0),
        'cache_v_sb': nrm(ks[4], sb_shape, 1.0),
        'cache_k_fox': nrm(ks[5], fox_shape, 1.0),
        'cache_v_fox': nrm(ks[6], fox_shape, 1.0),
        'cache_logf_fox': jax.nn.log_sigmoid(FORGET_BIAS_INIT + nrm(ks[7], (N_FOX_LAYERS, n_pool, PAGE_SIZE, N_HEADS), 0.5)),
        'page_table': page_table,
        'c_prompt': nrm(ks[8], (BATCH, D), 1.0),
        'c_sample': nrm(ks[9], (DEC_BATCH, D), 1.0),
        'ada_w': nrm(ks[14], (DEPTH, D, 6 * D), ADA_INIT * D ** -0.5),
        'ada_b': nrm(ks[15], (DEPTH, 6 * D), 0.02),
        'ln_g': 1.0 + nrm(ks[16], (DEPTH, 2, D), 0.02),
        'ln_b': nrm(ks[17], (DEPTH, 2, D), 0.02),
        'w_in_sb': w_in_sb,
        'w_out_sb': nrm(ks[11], (N_SB_LAYERS, D, D), DEEPNORM_BETA * D ** -0.5),
        'w_in_fox': w_in_fox,
        'b_f_fox': FORGET_BIAS_INIT + nrm(ks[13], (N_FOX_LAYERS, N_HEADS), 0.5),
        'w_out_fox': nrm(ks[18], (N_FOX_LAYERS, D, D), DEEPNORM_BETA * D ** -0.5),
        'router_w': nrm(ks[19], (D, N_EXPERTS), D ** -0.5),
        'router_b': nrm(ks[20], (N_EXPERTS,), 0.01),
        'w_gate': nrm(ks[21], (DEPTH, N_EXPERTS, D, D_EXPERT), D ** -0.5),
        'w_up': nrm(ks[22], (DEPTH, N_EXPERTS, D, D_EXPERT), D ** -0.5),
        'w_down': nrm(ks[23], (DEPTH, N_EXPERTS, D_EXPERT, D), DEEPNORM_BETA * D_EXPERT ** -0.5),
    }


def reference(x_prompt, x_sample, cache_k_sb, cache_v_sb, cache_k_fox, cache_v_fox, cache_logf_fox,
              page_table, c_prompt, c_sample, ada_w, ada_b, ln_g, ln_b, w_in_sb, w_out_sb,
              w_in_fox, b_f_fox, w_out_fox, router_w, router_b, w_gate, w_up, w_down):
    y_prompt, sb_p, fox_p = _run_trunk(
        x_prompt, c_prompt, None, None, None, None, None, None,
        ada_w, ada_b, ln_g, ln_b, w_in_sb, w_out_sb, w_in_fox, b_f_fox, w_out_fox,
        router_w, router_b, w_gate, w_up, w_down)
    y_sample, sb_s, fox_s = _run_trunk(
        x_sample, c_sample, page_table, cache_k_sb, cache_v_sb, cache_k_fox, cache_v_fox, cache_logf_fox,
        ada_w, ada_b, ln_g, ln_b, w_in_sb, w_out_sb, w_in_fox, b_f_fox, w_out_fox,
        router_w, router_b, w_gate, w_up, w_down)
    k_sb_prompt = jnp.stack([e[0] for e in sb_p])
    v_sb_prompt = jnp.stack([e[1] for e in sb_p])
    k_sb_sample = jnp.stack([e[0] for e in sb_s])
    v_sb_sample = jnp.stack([e[1] for e in sb_s])
    k_fox_prompt = jnp.stack([e[0] for e in fox_p])
    v_fox_prompt = jnp.stack([e[1] for e in fox_p])
    logf_fox_prompt = jnp.stack([e[2] for e in fox_p])
    k_fox_sample = jnp.stack([e[0] for e in fox_s])
    v_fox_sample = jnp.stack([e[1] for e in fox_s])
    logf_fox_sample = jnp.stack([e[2] for e in fox_s])
    return (y_prompt, y_sample, k_sb_prompt, v_sb_prompt, k_sb_sample, v_sb_sample,
            k_fox_prompt, v_fox_prompt, logf_fox_prompt, k_fox_sample, v_fox_sample, logf_fox_sample)
```

```python
import functools

import jax
import jax.numpy as jnp
from jax import lax
from jax.experimental import pallas as pl
from jax.experimental.pallas import tpu as pltpu

F32 = jnp.float32
BF16 = jnp.bfloat16

N_GROUPS = 4
TOP_K = 2
LN_EPS = 1e-5
NEG_BIG = -1e30

TOKEN_TILE = 256
MOE_ROW_TILE = 256
ATT_Q_TILE = 128
ATT_KV_TILE = 256
LANES = 128
V7X_VMEM_LIMIT = 56 * 1024 * 1024


def _params(*sem):
    return pltpu.CompilerParams(dimension_semantics=sem, vmem_limit_bytes=V7X_VMEM_LIMIT)


def _log_sigmoid_neg(z):
    return -(jnp.maximum(z, 0.0) + jnp.log(1.0 + jnp.exp(-jnp.abs(z))))


def _split_dot(a, tri):
    hi = a.astype(BF16)
    lo = (a - hi.astype(F32)).astype(BF16)
    return (jnp.dot(hi, tri, preferred_element_type=F32)
            + jnp.dot(lo, tri, preferred_element_type=F32))


def _tri_incl(n):
    r = lax.broadcasted_iota(jnp.int32, (n, n), 0)
    c = lax.broadcasted_iota(jnp.int32, (n, n), 1)
    return (r >= c).astype(BF16)


def _tri_strict(n):
    r = lax.broadcasted_iota(jnp.int32, (n, n), 0)
    c = lax.broadcasted_iota(jnp.int32, (n, n), 1)
    return (r > c).astype(F32)


def _mod_body(c_ref, w_ref, b_ref, o_ref):
    c = c_ref[...]
    a = c / (1.0 + jnp.exp(-c))
    o_ref[0] = jnp.dot(a.astype(BF16), w_ref[0].astype(BF16),
                       preferred_element_type=F32) + b_ref[0]


def _modulation(c_rows, ada_w, ada_b):
    depth, d, d6 = ada_w.shape
    rows = c_rows.shape[0]
    tn = min(1024, d6)
    return pl.pallas_call(
        _mod_body,
        grid=(depth, d6 // tn),
        in_specs=[pl.BlockSpec((rows, d), lambda l, j: (0, 0)),
                  pl.BlockSpec((1, d, tn), lambda l, j: (l, 0, j)),
                  pl.BlockSpec((1, 1, tn), lambda l, j: (l, 0, j))],
        out_specs=pl.BlockSpec((1, rows, tn), lambda l, j: (l, 0, j)),
        out_shape=jax.ShapeDtypeStruct((depth, rows, d6), F32),
        compiler_params=_params("arbitrary", "arbitrary"),
        name="modulation",
    )(c_rows, ada_w, ada_b.reshape(depth, 1, d6))


def _qkv_body(*refs, nq, nkv, scale, fox):
    if fox:
        (x_ref, sh_ref, sc_ref, w_ref, wf_ref, bf_ref,
         q_ref, k_ref, v_ref, kb_ref, vb_ref, lf_ref) = refs
    else:
        x_ref, sh_ref, sc_ref, w_ref, q_ref, k_ref, v_ref, kb_ref, vb_ref = refs
    h = x_ref[...] * (1.0 + sc_ref[0]) + sh_ref[0]
    p = jnp.dot(h.astype(BF16), w_ref[...], preferred_element_type=F32)
    q_ref[...] = (p[:, :nq] * scale).astype(BF16)
    k = p[:, nq:nq + nkv]
    v = p[:, nq + nkv:nq + 2 * nkv]
    k_ref[...] = k
    v_ref[...] = v
    kb_ref[...] = k.astype(BF16)
    vb_ref[...] = v.astype(BF16)
    if fox:
        gpre = jnp.dot(h, wf_ref[...], preferred_element_type=F32,
                       precision=lax.Precision.HIGHEST) + bf_ref[...]
        lf_ref[...] = _log_sigmoid_neg(-gpre)


def _qkv_proj(x, mod, layer, w_main, w_f, b_f, *, n_prompt_tiles, nq, nkv, scale):
    n, d = x.shape
    tm = TOKEN_TILE
    fox = w_f is not None
    kind = lambda i: i // n_prompt_tiles
    in_specs = [pl.BlockSpec((tm, d), lambda i: (i, 0)),
                pl.BlockSpec((1, tm, d), lambda i: (layer, kind(i), 0)),
                pl.BlockSpec((1, tm, d), lambda i: (layer, kind(i), 1)),
                pl.BlockSpec(w_main.shape, lambda i: (0, 0))]
    args = [x, mod, mod, w_main]
    out_specs = [pl.BlockSpec((tm, nq), lambda i: (i, 0)),
                 pl.BlockSpec((tm, nkv), lambda i: (i, 0)),
                 pl.BlockSpec((tm, nkv), lambda i: (i, 0)),
                 pl.BlockSpec((tm, nkv), lambda i: (i, 0)),
                 pl.BlockSpec((tm, nkv), lambda i: (i, 0))]
    out_shape = [jax.ShapeDtypeStruct((n, nq), BF16),
                 jax.ShapeDtypeStruct((n, nkv), F32),
                 jax.ShapeDtypeStruct((n, nkv), F32),
                 jax.ShapeDtypeStruct((n, nkv), BF16),
                 jax.ShapeDtypeStruct((n, nkv), BF16)]
    if fox:
        in_specs += [pl.BlockSpec(w_f.shape, lambda i: (0, 0)),
                     pl.BlockSpec(b_f.shape, lambda i: (0, 0))]
        args += [w_f, b_f]
        out_specs.append(pl.BlockSpec((tm, LANES), lambda i: (i, 0)))
        out_shape.append(jax.ShapeDtypeStruct((n, LANES), F32))
    return pl.pallas_call(
        functools.partial(_qkv_body, nq=nq, nkv=nkv, scale=scale, fox=fox),
        grid=(n // tm,),
        in_specs=in_specs, out_specs=out_specs, out_shape=out_shape,
        compiler_params=_params("arbitrary"),
        name="qkv_fox" if fox else "qkv_sb",
    )(*args)


def _stack_heads(q, g, hd):
    return jnp.concatenate([q[:, i * hd:(i + 1) * hd] for i in range(g)], axis=0)


def _unstack_heads(o, g, tq):
    return jnp.concatenate([o[i * tq:(i + 1) * tq] for i in range(g)], axis=1)


def _sb_prompt_body(q_ref, k_ref, v_ref, o_ref, *, tq, tk, g, hd):
    qi = pl.program_id(1)
    rows = g * tq
    qs = _stack_heads(q_ref[...], g, hd)
    jd = (qi * tq) // tk
    tri = _tri_incl(tk)
    r_io = lax.broadcasted_iota(jnp.int32, (rows, tk), 0)
    c_io = lax.broadcasted_iota(jnp.int32, (rows, tk), 1)
    t_pos = qi * tq + (r_io & (tq - 1))

    def block(j, run, acc, masked):
        start = pl.multiple_of(j * tk, tk)
        kb = k_ref[pl.ds(start, tk), :]
        vb = v_ref[pl.ds(start, tk), :]
        z = lax.dot_general(qs, kb, (((1,), (1,)), ((), ())), preferred_element_type=F32)
        lg = _log_sigmoid_neg(z)
        if masked:
            valid = (j * tk + c_io) < t_pos
            lg = jnp.where(valid, lg, 0.0)
        c = _split_dot(lg, tri)
        w = jnp.exp(z + c + run)
        if masked:
            w = jnp.where(valid, w, 0.0)
        acc = acc + jnp.dot(w.astype(BF16), vb, preferred_element_type=F32)
        return run + c[:, 0:1], acc

    run0 = jnp.zeros((rows, 1), F32)
    acc0 = jnp.zeros((rows, hd), F32)
    run, acc = block(jd, run0, acc0, True)
    run, acc = lax.fori_loop(0, jd, lambda i, c: block(jd - 1 - i, c[0], c[1], False), (run, acc))
    o_ref[...] = _unstack_heads(acc, g, tq).astype(BF16)


def _fox_prompt_body(q_ref, k_ref, v_ref, ck_ref, o_ref, *, tq, tk, g, hd):
    h = pl.program_id(0)
    qi = pl.program_id(1)
    rows = g * tq
    qs = _stack_heads(q_ref[...], g, hd)
    jd = (qi * tq) // tk
    r_io = lax.broadcasted_iota(jnp.int32, (rows, tk), 0)
    c_io = lax.broadcasted_iota(jnp.int32, (rows, tk), 1)
    t_pos = qi * tq + (r_io & (tq - 1))

    def block(j, m, l, acc, masked):
        start = pl.multiple_of(j * tk, tk)
        kb = k_ref[pl.ds(start, tk), :]
        vb = v_ref[pl.ds(start, tk), :]
        z = lax.dot_general(qs, kb, (((1,), (1,)), ((), ())), preferred_element_type=F32)
        bias = jnp.concatenate(
            [jnp.broadcast_to(ck_ref[pl.ds(h * g + i, 1), pl.ds(start, tk)], (tq, tk))
             for i in range(g)], axis=0)
        s = z - bias
        if masked:
            s = jnp.where((j * tk + c_io) <= t_pos, s, NEG_BIG)
        m_new = jnp.maximum(m, jnp.max(s, axis=-1, keepdims=True))
        alpha = jnp.exp(m - m_new)
        p = jnp.exp(s - m_new)
        l = alpha * l + jnp.sum(p, axis=-1, keepdims=True)
        acc = alpha * acc + jnp.dot(p.astype(BF16), vb, preferred_element_type=F32)
        return m_new, l, acc

    m0 = jnp.full((rows, 1), NEG_BIG, F32)
    l0 = jnp.zeros((rows, 1), F32)
    acc0 = jnp.zeros((rows, hd), F32)
    m, l, acc = block(jd, m0, l0, acc0, True)
    m, l, acc = lax.fori_loop(0, jd, lambda i, c: block(jd - 1 - i, c[0], c[1], c[2], False),
                              (m, l, acc))
    o_ref[...] = _unstack_heads(acc / l, g, tq).astype(BF16)


def _prompt_attention(q, kb, vb, ck_t, *, seq, n_kv, g, hd):
    tq, tk = ATT_Q_TILE, ATT_KV_TILE
    fox = ck_t is not None
    in_specs = [pl.BlockSpec((tq, g * hd), lambda h, i: (i, h)),
                pl.BlockSpec((seq, hd), lambda h, i: (0, h)),
                pl.BlockSpec((seq, hd), lambda h, i: (0, h))]
    args = [q, kb, vb]
    if fox:
        in_specs.append(pl.BlockSpec(ck_t.shape, lambda h, i: (0, 0)))
        args.append(ck_t)
    body = _fox_prompt_body if fox else _sb_prompt_body
    return pl.pallas_call(
        functools.partial(body, tq=tq, tk=tk, g=g, hd=hd),
        grid=(n_kv, seq // tq),
        in_specs=in_specs,
        out_specs=pl.BlockSpec((tq, g * hd), lambda h, i: (i, h)),
        out_shape=jax.ShapeDtypeStruct((seq, n_kv * g * hd), BF16),
        compiler_params=_params("arbitrary", "arbitrary"),
        name="fox_prompt_attention" if fox else "sb_prompt_attention",
    )(*args)


def _cumsum_body(lf_ref, o_ref, carry_ref, *, tk):
    @pl.when(pl.program_id(0) == 0)
    def _():
        carry_ref[...] = jnp.zeros_like(carry_ref)
    r = lax.broadcasted_iota(jnp.int32, (tk, tk), 0)
    c = lax.broadcasted_iota(jnp.int32, (tk, tk), 1)
    upper = (r <= c).astype(F32)
    cum = jnp.dot(lf_ref[...], upper, preferred_element_type=F32,
                  precision=lax.Precision.HIGHEST) + carry_ref[:, 0:1]
    o_ref[...] = cum
    carry_ref[...] = jnp.broadcast_to(cum[:, tk - 1:tk], carry_ref.shape)


def _cumsum_lanes(lf_t):
    nh, seq = lf_t.shape
    tk = ATT_KV_TILE
    return pl.pallas_call(
        functools.partial(_cumsum_body, tk=tk),
        grid=(seq // tk,),
        in_specs=[pl.BlockSpec((nh, tk), lambda j: (0, j))],
        out_specs=pl.BlockSpec((nh, tk), lambda j: (0, j)),
        out_shape=jax.ShapeDtypeStruct((nh, seq), F32),
        scratch_shapes=[pltpu.VMEM((nh, LANES), F32)],
        compiler_params=_params("arbitrary"),
        name="logf_cumsum",
    )(lf_t)


def _sample_att_body(pt_ref, q_ref, kn_ref, vn_ref, *rest, n_pages, page, n_kv, g, hd, t_new, fox):
    if fox:
        lfn_ref, kpool, vpool, lfpool, o_ref, kbuf, vbuf, lfbuf, sem = rest
    else:
        kpool, vpool, o_ref, kbuf, vbuf, sem = rest
    b = pl.program_id(0)
    nb = pl.num_programs(0)
    slot = b % 2
    past = n_pages * page
    rows = n_kv * g * t_new

    def copies(bb, sl):
        out = []
        for p in range(n_pages):
            pid = pt_ref[bb, p]
            out.append(pltpu.make_async_copy(kpool.at[pid], kbuf.at[sl, p], sem.at[0, sl]))
            out.append(pltpu.make_async_copy(vpool.at[pid], vbuf.at[sl, p], sem.at[1, sl]))
            if fox:
                out.append(pltpu.make_async_copy(lfpool.at[pid], lfbuf.at[sl, p], sem.at[2, sl]))
        return out

    @pl.when(b == 0)
    def _():
        for cp in copies(0, 0):
            cp.start()

    @pl.when(b + 1 < nb)
    def _():
        for cp in copies(b + 1, 1 - slot):
            cp.start()

    for cp in copies(b, slot):
        cp.wait()

    q = q_ref[0]
    kn = jnp.concatenate([kn_ref[0], jnp.zeros((LANES - t_new, n_kv * hd), F32)], axis=0)
    vn = jnp.concatenate([vn_ref[0], jnp.zeros((LANES - t_new, n_kv * hd), F32)], axis=0)

    z_parts, zn_parts = [], []
    for h in range(n_kv):
        qh = jnp.concatenate([q[:, (h * g + i) * hd:(h * g + i + 1) * hd] for i in range(g)], axis=0)
        kh = kbuf[slot, :, :, h * hd:(h + 1) * hd].reshape(past, hd).astype(BF16)
        z_parts.append(lax.dot_general(qh, kh, (((1,), (1,)), ((), ())), preferred_element_type=F32))
        knh = kn[:, h * hd:(h + 1) * hd].astype(BF16)
        zn_parts.append(lax.dot_general(qh, knh, (((1,), (1,)), ((), ())), preferred_element_type=F32))
    z = jnp.concatenate(z_parts, axis=0)
    zn = jnp.concatenate(zn_parts, axis=0)

    r_io = lax.broadcasted_iota(jnp.int32, (rows, LANES), 0)
    c_io = lax.broadcasted_iota(jnp.int32, (rows, LANES), 1)
    t_row = r_io % t_new

    if not fox:
        valid_n = c_io < t_row
        lgn = jnp.where(valid_n, _log_sigmoid_neg(zn), 0.0)
        cn = _split_dot(lgn, _tri_incl(LANES))
        wn = jnp.where(valid_n, jnp.exp(zn + cn), 0.0)
        run = cn[:, 0:1]
        tk = ATT_KV_TILE
        tri = _tri_incl(tk)
        w_parts = [None] * (past // tk)
        for j in reversed(range(past // tk)):
            zb = z[:, j * tk:(j + 1) * tk]
            c = _split_dot(_log_sigmoid_neg(zb), tri)
            w_parts[j] = jnp.exp(zb + c + run)
            run = run + c[:, 0:1]
        w = jnp.concatenate(w_parts, axis=1)
        denom = None
    else:
        lfn = lfn_ref[0]
        nh = n_kv * g
        tri = _tri_strict(page)
        bias_n = jnp.dot(lfn, tri, preferred_element_type=F32, precision=lax.Precision.HIGHEST)
        carry = jnp.sum(lfn, axis=-1, keepdims=True)
        bias_parts = [None] * n_pages
        for p in reversed(range(n_pages)):
            lfp = lfbuf[slot, p]
            bias_parts[p] = jnp.dot(lfp, tri, preferred_element_type=F32,
                                    precision=lax.Precision.HIGHEST) + carry
            carry = carry + jnp.sum(lfp, axis=-1, keepdims=True)
        bias_h = jnp.concatenate(bias_parts, axis=1)
        bias = jnp.concatenate([jnp.broadcast_to(bias_h[i:i + 1], (t_new, past)) for i in range(nh)], axis=0)
        bias_new = jnp.concatenate([jnp.broadcast_to(bias_n[i:i + 1], (t_new, LANES)) for i in range(nh)], axis=0)
        s = z + bias
        sn = jnp.where(c_io <= t_row, zn + bias_new, NEG_BIG)
        m = jnp.maximum(jnp.max(s, axis=-1, keepdims=True), jnp.max(sn, axis=-1, keepdims=True))
        w = jnp.exp(s - m)
        wn = jnp.exp(sn - m)
        denom = jnp.sum(w, axis=-1, keepdims=True) + jnp.sum(wn, axis=-1, keepdims=True)

    wb = w.astype(BF16)
    wnb = wn.astype(BF16)
    o_cols = []
    rh = g * t_new
    for h in range(n_kv):
        vh = vbuf[slot, :, :, h * hd:(h + 1) * hd].reshape(past, hd).astype(BF16)
        vnh = vn[:, h * hd:(h + 1) * hd].astype(BF16)
        oh = (jnp.dot(wb[h * rh:(h + 1) * rh], vh, preferred_element_type=F32)
              + jnp.dot(wnb[h * rh:(h + 1) * rh], vnh, preferred_element_type=F32))
        if fox:
            oh = oh / denom[h * rh:(h + 1) * rh]
        for i in range(g):
            o_cols.append(oh[i * t_new:(i + 1) * t_new])
    o_ref[0] = jnp.concatenate(o_cols, axis=1).astype(BF16)


def _sample_attention(page_table, q_b, k_new, v_new, k_pool, v_pool, lf_new_t, lf_pool_t, *, n_kv, g, hd):
    nb, t_new, dq = q_b.shape
    n_pages = page_table.shape[1]
    page = k_pool.shape[1]
    nkv = n_kv * hd
    fox = lf_pool_t is not None
    in_specs = [pl.BlockSpec((1, t_new, dq), lambda b, pt: (b, 0, 0)),
                pl.BlockSpec((1, t_new, nkv), lambda b, pt: (b, 0, 0)),
                pl.BlockSpec((1, t_new, nkv), lambda b, pt: (b, 0, 0))]
    args = [q_b, k_new, v_new]
    scratch = [pltpu.VMEM((2, n_pages, page, nkv), F32),
               pltpu.VMEM((2, n_pages, page, nkv), F32)]
    if fox:
        nh = n_kv * g
        in_specs.append(pl.BlockSpec((1, nh, LANES), lambda b, pt: (b, 0, 0)))
        args.append(lf_new_t)
    in_specs += [pl.BlockSpec(memory_space=pl.ANY), pl.BlockSpec(memory_space=pl.ANY)]
    args += [k_pool, v_pool]
    if fox:
        in_specs.append(pl.BlockSpec(memory_space=pl.ANY))
        args.append(lf_pool_t)
        scratch.append(pltpu.VMEM((2, n_pages, nh, page), F32))
    scratch.append(pltpu.SemaphoreType.DMA((3, 2)))
    return pl.pallas_call(
        functools.partial(_sample_att_body, n_pages=n_pages, page=page, n_kv=n_kv, g=g, hd=hd,
                          t_new=t_new, fox=fox),
        grid_spec=pltpu.PrefetchScalarGridSpec(
            num_scalar_prefetch=1, grid=(nb,), in_specs=in_specs,
            out_specs=pl.BlockSpec((1, t_new, dq), lambda b, pt: (b, 0, 0)),
            scratch_shapes=scratch),
        out_shape=jax.ShapeDtypeStruct((nb, t_new, dq), BF16),
        compiler_params=_params("arbitrary"),
        name="fox_sample_attention" if fox else "sb_sample_attention",
    )(page_table, *args)


def _layernorm(u, gamma, beta):
    mu = jnp.mean(u, axis=-1, keepdims=True)
    uc = u - mu
    var = jnp.mean(uc * uc, axis=-1, keepdims=True)
    return uc * lax.rsqrt(var + LN_EPS) * gamma + beta


def _first_index_of_max(vals, mask, lane):
    top = jnp.max(jnp.where(mask, vals, NEG_BIG), axis=-1, keepdims=True)
    idx = jnp.min(jnp.where(mask & (vals == top), lane, 2 * LANES), axis=-1, keepdims=True)
    return top, idx


def _post_attn_body(o_ref, x_ref, g1_ref, sh2_ref, sc2_ref, w_ref, lng_ref, lnb_ref, rw_ref, rb_ref,
                    x1_ref, h2_ref, eid_ref, gate_ref, *, alpha, n_exp):
    y = jnp.dot(o_ref[...], w_ref[...], preferred_element_type=F32)
    x1 = _layernorm(alpha * x_ref[...] + g1_ref[0] * y, lng_ref[...], lnb_ref[...])
    x1_ref[...] = x1
    h2 = x1 * (1.0 + sc2_ref[0]) + sh2_ref[0]
    h2_ref[...] = h2

    logits = jnp.dot(h2, rw_ref[...], preferred_element_type=F32, precision=lax.Precision.HIGHEST)
    lane = lax.broadcasted_iota(jnp.int32, logits.shape, 1)
    real = lane < n_exp
    logits = jnp.where(real, logits, NEG_BIG)
    e = jnp.exp(logits - jnp.max(logits, axis=-1, keepdims=True))
    probs = e / jnp.sum(e, axis=-1, keepdims=True)
    sel = probs + rb_ref[...]
    epg = n_exp // N_GROUPS
    best = first = second = None
    for grp in range(N_GROUPS):
        in_g = (lane >= grp * epg) & (lane < (grp + 1) * epg)
        top1, idx1 = _first_index_of_max(sel, in_g, lane)
        top2, idx2 = _first_index_of_max(sel, in_g & (lane != idx1), lane)
        score = top1 + top2
        if grp == 0:
            best, first, second = score, idx1, idx2
        else:
            better = score > best
            best = jnp.where(better, score, best)
            first = jnp.where(better, idx1, first)
            second = jnp.where(better, idx2, second)
    gate1 = jnp.sum(jnp.where(lane == first, probs, 0.0), axis=-1, keepdims=True)
    gate2 = jnp.sum(jnp.where(lane == second, probs, 0.0), axis=-1, keepdims=True)
    tot = gate1 + gate2
    eid_ref[...] = jnp.where(lane == 0, first, jnp.where(lane == 1, second, 0))
    gate_ref[...] = jnp.where(lane == 0, gate1 / tot, jnp.where(lane == 1, gate2 / tot, 0.0))


def _post_attention(o, x, mod, layer, w_out, ln_g, ln_b, rw_pad, rb_pad, *, n_prompt_tiles, alpha, n_exp):
    n, d = x.shape
    tm = TOKEN_TILE
    kind = lambda i: i // n_prompt_tiles
    row = pl.BlockSpec((tm, d), lambda i: (i, 0))
    modspec = lambda c: pl.BlockSpec((1, tm, d), lambda i: (layer, kind(i), c))
    const = lambda a: pl.BlockSpec(a.shape, lambda i: (0,) * a.ndim)
    lanes = pl.BlockSpec((tm, LANES), lambda i: (i, 0))
    return pl.pallas_call(
        functools.partial(_post_attn_body, alpha=alpha, n_exp=n_exp),
        grid=(n // tm,),
        in_specs=[row, row, modspec(2), modspec(3), modspec(4), const(w_out), const(ln_g), const(ln_b),
                  const(rw_pad), const(rb_pad)],
        out_specs=[row, row, lanes, lanes],
        out_shape=[jax.ShapeDtypeStruct((n, d), F32), jax.ShapeDtypeStruct((n, d), F32),
                   jax.ShapeDtypeStruct((n, LANES), jnp.int32), jax.ShapeDtypeStruct((n, LANES), F32)],
        compiler_params=_params("arbitrary"),
        name="post_attention",
    )(o, x, mod, mod, mod, w_out, ln_g, ln_b, rw_pad, rb_pad)


def _moe_body(texp_ref, trows_ref, src_ref, dst_ref, srcn_ref, gate_ref, wg_ref, wu_ref, wd_ref,
              h_hbm, out_hbm, xbuf, ybuf, sem, *, tm):
    t = pl.program_id(0)
    nt = pl.num_programs(0)
    slot = t % 2
    del texp_ref
    rows_of = lambda i: trows_ref[jnp.clip(i, 0, nt - 1)]

    def gather(idx_ref, sl, start):
        def row(r, _):
            cp = pltpu.make_async_copy(h_hbm.at[idx_ref[0, 0, r]], xbuf.at[sl, r], sem.at[0, sl])
            if start:
                cp.start()
            else:
                cp.wait()
            return 0
        lax.fori_loop(0, tm, row, 0)

    def scatter(sl, n_rows, start):
        def row(r, _):
            cp = pltpu.make_async_copy(ybuf.at[sl, r], out_hbm.at[dst_ref[0, 0, r]], sem.at[1, sl])
            if start:
                cp.start()
            else:
                cp.wait()
            return 0
        lax.fori_loop(0, n_rows, row, 0)

    @pl.when((t == 0) & (rows_of(0) > 0))
    def _():
        gather(src_ref, 0, True)

    @pl.when((t + 1 < nt) & (rows_of(t + 1) > 0))
    def _():
        gather(srcn_ref, 1 - slot, True)

    @pl.when(t >= 2)
    def _():
        scatter(slot, rows_of(t - 2), False)

    @pl.when(rows_of(t) > 0)
    def _():
        gather(src_ref, slot, False)
        xb = xbuf[slot].astype(BF16)
        gp = jnp.dot(xb, wg_ref[0, 0], preferred_element_type=F32)
        up = jnp.dot(xb, wu_ref[0, 0], preferred_element_type=F32)
        act = gp / (1.0 + jnp.exp(-gp)) * up * gate_ref[...]
        ybuf[slot] = jnp.dot(act.astype(BF16), wd_ref[0, 0], preferred_element_type=F32)
        scatter(slot, rows_of(t), True)

    @pl.when(t == nt - 1)
    def _():
        @pl.when(nt >= 2)
        def _():
            scatter(1 - slot, rows_of(t - 1), False)
        scatter(slot, rows_of(t), False)


def _moe(h2, tile_expert, tile_rows, row_src, row_dst, row_gate, w_gate, w_up, w_down, layer):
    n, d = h2.shape
    tm = MOE_ROW_TILE
    n_tiles = tile_expert.shape[0]
    f = w_gate.shape[-1]
    nxt = lambda t: jnp.minimum(t + 1, n_tiles - 1)
    smem_rows = lambda imap: pl.BlockSpec((1, 1, tm), imap, memory_space=pltpu.SMEM)
    return pl.pallas_call(
        functools.partial(_moe_body, tm=tm),
        grid_spec=pltpu.PrefetchScalarGridSpec(
            num_scalar_prefetch=2, grid=(n_tiles,),
            in_specs=[smem_rows(lambda t, te, tr: (t, 0, 0)),
                      smem_rows(lambda t, te, tr: (t, 0, 0)),
                      smem_rows(lambda t, te, tr: (nxt(t), 0, 0)),
                      pl.BlockSpec((tm, 1), lambda t, te, tr: (t, 0)),
                      pl.BlockSpec((1, 1, d, f), lambda t, te, tr: (layer, te[t], 0, 0)),
                      pl.BlockSpec((1, 1, d, f), lambda t, te, tr: (layer, te[t], 0, 0)),
                      pl.BlockSpec((1, 1, f, d), lambda t, te, tr: (layer, te[t], 0, 0)),
                      pl.BlockSpec(memory_space=pl.ANY)],
            out_specs=pl.BlockSpec(memory_space=pl.ANY),
            scratch_shapes=[pltpu.VMEM((2, tm, d), F32), pltpu.VMEM((2, tm, d), F32),
                            pltpu.SemaphoreType.DMA((2, 2))]),
        out_shape=jax.ShapeDtypeStruct((TOP_K * n, d), F32),
        compiler_params=_params("arbitrary"),
        name="sparse_experts",
    )(tile_expert, tile_rows, row_src, row_dst, row_src, row_gate, w_gate, w_up, w_down, h2)


def _routing_tables(eid, gate, n_exp, tm):
    n = eid.shape[0]
    n_assign = n * TOP_K
    n_tiles = (n_assign + n_exp * (tm - 1)) // tm + 1
    e_flat = eid.reshape(n_assign)
    order = jnp.argsort(e_flat, stable=True).astype(jnp.int32)
    counts = jnp.sum((e_flat[:, None] == jnp.arange(n_exp, dtype=jnp.int32)[None, :]).astype(jnp.int32), axis=0)
    starts = jnp.cumsum(counts) - counts
    tiles_per = (counts + tm - 1) // tm
    tile_starts = jnp.cumsum(tiles_per) - tiles_per
    n_used = jnp.sum(tiles_per).astype(jnp.int32)
    tile_ids = jnp.arange(n_tiles, dtype=jnp.int32)
    tile_expert = jnp.clip(jnp.searchsorted(tile_starts + tiles_per, tile_ids, side="right"), 0, n_exp - 1)
    tile_expert = tile_expert.astype(jnp.int32)
    row_in_tile = jnp.arange(tm, dtype=jnp.int32)[None, :]
    within = (tile_ids - tile_starts[tile_expert])[:, None] * tm + row_in_tile
    valid = (within < counts[tile_expert][:, None]) & (tile_ids[:, None] < n_used)
    sorted_pos = jnp.clip(starts[tile_expert][:, None] + within, 0, n_assign - 1)
    assign = order[sorted_pos]
    tok = assign // TOP_K
    slot_k = assign % TOP_K
    row_src = jnp.where(valid, tok, 0).astype(jnp.int32)
    row_dst = jnp.where(valid, slot_k * n + tok, 0).astype(jnp.int32)
    row_gate = jnp.where(valid, gate.reshape(n_assign)[assign], 0.0).astype(F32)
    tile_rows = jnp.sum(valid.astype(jnp.int32), axis=1)
    return (tile_expert, tile_rows, row_src.reshape(n_tiles, 1, tm),
            row_dst.reshape(n_tiles, 1, tm), row_gate.reshape(n_tiles * tm, 1))


def _combine_body(x1_ref, ya_ref, yb_ref, g2_ref, lng_ref, lnb_ref, o_ref, *, alpha):
    y = ya_ref[...] + yb_ref[...]
    o_ref[...] = _layernorm(alpha * x1_ref[...] + g2_ref[0] * y, lng_ref[...], lnb_ref[...])


def _combine(x1, y2, mod, layer, ln_g, ln_b, *, n_prompt_tiles, alpha):
    n, d = x1.shape
    tm = TOKEN_TILE
    kind = lambda i: i // n_prompt_tiles
    row = pl.BlockSpec((tm, d), lambda i: (i, 0))
    const = lambda a: pl.BlockSpec(a.shape, lambda i: (0,) * a.ndim)
    return pl.pallas_call(
        functools.partial(_combine_body, alpha=alpha),
        grid=(n // tm,),
        in_specs=[row, row, pl.BlockSpec((tm, d), lambda i: (n // tm + i, 0)),
                  pl.BlockSpec((1, tm, d), lambda i: (layer, kind(i), 5)), const(ln_g), const(ln_b)],
        out_specs=row,
        out_shape=jax.ShapeDtypeStruct((n, d), F32),
        compiler_params=_params("arbitrary"),
        name="combine_layernorm",
    )(x1, y2, y2, mod, ln_g, ln_b)


def kernel(x_prompt, x_sample, cache_k_sb, cache_v_sb, cache_k_fox, cache_v_fox, cache_logf_fox, page_table,
           c_prompt, c_sample, ada_w, ada_b, ln_g, ln_b, w_in_sb, w_out_sb, w_in_fox, b_f_fox, w_out_fox,
           router_w, router_b, w_gate, w_up, w_down):
    batch, seq, d = x_prompt.shape
    nb, t_new, _ = x_sample.shape
    depth = ada_w.shape[0]
    n_pool, page, n_kv, hd = cache_k_sb.shape[1:]
    n_heads = d // hd
    g = n_heads // n_kv
    nq, nkv = n_heads * hd, n_kv * hd
    n_exp = router_w.shape[1]
    scale = float(hd) ** -0.5
    alpha = (2.0 * depth) ** 0.25
    tm = TOKEN_TILE
    n_p, n_s = batch * seq, nb * t_new
    n = n_p + n_s
    assert batch == 1 and n_p % tm == 0 and n_s % tm == 0 and tm % nb == 0
    assert seq % ATT_KV_TILE == 0 and (page * page_table.shape[1]) % ATT_KV_TILE == 0
    assert n_exp % N_GROUPS == 0 and n_exp <= LANES and n_heads <= LANES and t_new <= LANES
    n_prompt_tiles = n_p // tm

    xs_t = jnp.transpose(x_sample, (1, 0, 2)).reshape(n_s, d)
    x = jnp.concatenate([x_prompt.reshape(n_p, d), xs_t], axis=0)
    c_rows = jnp.concatenate([jnp.broadcast_to(c_prompt, (tm, d))] + [c_sample] * (tm // nb), axis=0)
    mod = _modulation(c_rows, ada_w, ada_b)

    rw_pad = jnp.pad(router_w, ((0, 0), (0, LANES - n_exp)))
    rb_pad = jnp.pad(router_b.reshape(1, n_exp), ((0, 0), (0, LANES - n_exp)))
    wg_b, wu_b, wd_b = w_gate.astype(BF16), w_up.astype(BF16), w_down.astype(BF16)

    def to_seq_major(a):
        return jnp.transpose(a.reshape(t_new, nb, a.shape[-1]), (1, 0, 2))

    def to_pos_major(a):
        return jnp.transpose(a, (1, 0, 2)).reshape(n_s, a.shape[-1])

    sb_out, fox_out = [], []
    for layer in range(depth):
        j = layer // 2
        fox = layer % 2 == 1
        if fox:
            w_in, w_out = w_in_fox[j], w_out_fox[j]
            w_main = w_in[:, :nq + 2 * nkv].astype(BF16)
            w_f = jnp.pad(w_in[:, nq + 2 * nkv:], ((0, 0), (0, LANES - n_heads)))
            b_f = jnp.pad(b_f_fox[j].reshape(1, n_heads), ((0, 0), (0, LANES - n_heads)))
            k_pool, v_pool = cache_k_fox[j], cache_v_fox[j]
        else:
            w_in, w_out = w_in_sb[j], w_out_sb[j]
            w_main, w_f, b_f = w_in.astype(BF16), None, None
            k_pool, v_pool = cache_k_sb[j], cache_v_sb[j]
        res = _qkv_proj(x, mod, layer, w_main, w_f, b_f, n_prompt_tiles=n_prompt_tiles,
                        nq=nq, nkv=nkv, scale=scale)
        q, k, v, kb, vb = res[:5]
        k_pool = k_pool.reshape(n_pool, page, nkv)
        v_pool = v_pool.reshape(n_pool, page, nkv)
        q_s = to_seq_major(q[n_p:])
        k_s, v_s = to_seq_major(k[n_p:]), to_seq_major(v[n_p:])
        if fox:
            lf = res[5][:, :n_heads]
            ck_t = _cumsum_lanes(jnp.transpose(lf[:n_p]))
            lf_s = to_seq_major(lf[n_p:])
            lf_new_t = jnp.pad(jnp.transpose(lf_s, (0, 2, 1)), ((0, 0), (0, 0), (0, LANES - t_new)))
            lf_pool_t = jnp.transpose(cache_logf_fox[j], (0, 2, 1))
            o_p = _prompt_attention(q, kb, vb, ck_t, seq=seq, n_kv=n_kv, g=g, hd=hd)
            o_s = _sample_attention(page_table, q_s, k_s, v_s, k_pool, v_pool, lf_new_t, lf_pool_t,
                                    n_kv=n_kv, g=g, hd=hd)
            fox_out.append((k[:n_p].reshape(batch, seq, n_kv, hd), v[:n_p].reshape(batch, seq, n_kv, hd),
                            lf[:n_p].reshape(batch, seq, n_heads),
                            k_s.reshape(nb, t_new, n_kv, hd), v_s.reshape(nb, t_new, n_kv, hd), lf_s))
        else:
            o_p = _prompt_attention(q, kb, vb, None, seq=seq, n_kv=n_kv, g=g, hd=hd)
            o_s = _sample_attention(page_table, q_s, k_s, v_s, k_pool, v_pool, None, None,
                                    n_kv=n_kv, g=g, hd=hd)
            sb_out.append((k[:n_p].reshape(batch, seq, n_kv, hd), v[:n_p].reshape(batch, seq, n_kv, hd),
                           k_s.reshape(nb, t_new, n_kv, hd), v_s.reshape(nb, t_new, n_kv, hd)))
        o = jnp.concatenate([o_p, to_pos_major(o_s)], axis=0)
        x1, h2, eid, gate = _post_attention(
            o, x, mod, layer, w_out.astype(BF16), ln_g[layer, 0].reshape(1, d), ln_b[layer, 0].reshape(1, d),
            rw_pad, rb_pad, n_prompt_tiles=n_prompt_tiles, alpha=alpha, n_exp=n_exp)
        tables = _routing_tables(eid[:, :TOP_K], gate[:, :TOP_K], n_exp, MOE_ROW_TILE)
        y2 = _moe(h2, *tables, wg_b, wu_b, wd_b, layer)
        x = _combine(x1, y2, mod, layer, ln_g[layer, 1].reshape(1, d), ln_b[layer, 1].reshape(1, d),
                     n_prompt_tiles=n_prompt_tiles, alpha=alpha)

    y_prompt = x[:n_p].reshape(batch, seq, d)
    y_sample = to_seq_major(x[n_p:])
    stack = lambda items, i: jnp.stack([e[i] for e in items])
    return (y_prompt, y_sample,
            stack(sb_out, 0), stack(sb_out, 1), stack(sb_out, 2), stack(sb_out, 3),
            stack(fox_out, 0), stack(fox_out, 1), stack(fox_out, 2),
            stack(fox_out, 3), stack(fox_out, 4), stack(fox_out, 5))
```

```python
import functools

import jax
import jax.numpy as jnp
from jax import lax
from jax.experimental import pallas as pl
from jax.experimental.pallas import tpu as pltpu

F32 = jnp.float32
BF16 = jnp.bfloat16

N_GROUPS = 4
TOP_K = 2
LN_EPS = 1e-5
NEG_BIG = -1e30

TOKEN_TILE = 256
MOE_ROW_TILE = 256
ROW_DMA_UNROLL = 8
ATT_Q_TILE = 256
ATT_KV_TILE = 256
LANES = 128
V7X_VMEM_LIMIT = 56 * 1024 * 1024


def _params(*sem):
    return pltpu.CompilerParams(dimension_semantics=sem, vmem_limit_bytes=V7X_VMEM_LIMIT)


LOG2E = 1.4426950408889634


def _log_sigmoid_neg(z):
    return -(jnp.maximum(z, 0.0) + jnp.log(1.0 + jnp.exp(-jnp.abs(z))))


def _softplus2(z2):
    neg_abs = pltpu.bitcast(pltpu.bitcast(z2, jnp.uint32) | jnp.uint32(0x80000000), F32)
    return jnp.maximum(z2, 0.0) + jnp.log2(1.0 + jnp.exp2(neg_abs))


def _cum_dot(a, tri):
    return jnp.dot(a.astype(BF16), tri, preferred_element_type=F32)


def _hi_lo_columns(w):
    w = jnp.pad(w, ((0, 0), (0, LANES - w.shape[1])))
    hi = w.astype(BF16)
    lo = (w - hi.astype(F32)).astype(BF16)
    return jnp.concatenate([hi, lo], axis=1)


def _dot_split(a, w_hl):
    hi = a.astype(BF16)
    lo = (a - hi.astype(F32)).astype(BF16)
    p = jnp.dot(hi, w_hl, preferred_element_type=F32)
    q = jnp.dot(lo, w_hl[:, :LANES], preferred_element_type=F32)
    return p[:, :LANES] + p[:, LANES:] + q


def _tri_incl(n):
    r = lax.broadcasted_iota(jnp.int32, (n, n), 0)
    c = lax.broadcasted_iota(jnp.int32, (n, n), 1)
    return (r >= c).astype(BF16)


def _tri_strict(n):
    r = lax.broadcasted_iota(jnp.int32, (n, n), 0)
    c = lax.broadcasted_iota(jnp.int32, (n, n), 1)
    return (r > c).astype(F32)


def _mod_body(c_ref, w_ref, b_ref, o_ref):
    c = c_ref[...]
    a = c / (1.0 + jnp.exp(-c))
    o_ref[0] = jnp.dot(a.astype(BF16), w_ref[0].astype(BF16),
                       preferred_element_type=F32) + b_ref[0]


def _modulation(c_rows, ada_w, ada_b):
    depth, d, d6 = ada_w.shape
    rows = c_rows.shape[0]
    tn = min(1024, d6)
    return pl.pallas_call(
        _mod_body,
        grid=(depth, d6 // tn),
        in_specs=[pl.BlockSpec((rows, d), lambda l, j: (0, 0)),
                  pl.BlockSpec((1, d, tn), lambda l, j: (l, 0, j)),
                  pl.BlockSpec((1, 1, tn), lambda l, j: (l, 0, j))],
        out_specs=pl.BlockSpec((1, rows, tn), lambda l, j: (l, 0, j)),
        out_shape=jax.ShapeDtypeStruct((depth, rows, d6), F32),
        compiler_params=_params("arbitrary", "arbitrary"),
        name="modulation",
    )(c_rows, ada_w, ada_b.reshape(depth, 1, d6))


def _qkv_body(*refs, nq, nkv, scale, fox):
    if fox:
        (x_ref, sh_ref, sc_ref, w_ref, wf_ref, bf_ref,
         q_ref, k_ref, v_ref, kb_ref, vb_ref, lf_ref) = refs
    else:
        x_ref, sh_ref, sc_ref, w_ref, q_ref, k_ref, v_ref, kb_ref, vb_ref = refs
    h = x_ref[...] * (1.0 + sc_ref[0]) + sh_ref[0]
    p = jnp.dot(h.astype(BF16), w_ref[...], preferred_element_type=F32)
    q_ref[...] = (p[:, :nq] * scale).astype(BF16)
    k = p[:, nq:nq + nkv]
    v = p[:, nq + nkv:nq + 2 * nkv]
    k_ref[...] = k
    v_ref[...] = v
    kb_ref[...] = k.astype(BF16)
    vb_ref[...] = v.astype(BF16)
    if fox:
        gpre = _dot_split(h, wf_ref[...]) + bf_ref[...]
        lf_ref[...] = _log_sigmoid_neg(-gpre)


def _qkv_proj(x, mod, layer, w_main, w_f, b_f, *, n_prompt_tiles, nq, nkv, scale):
    n, d = x.shape
    tm = TOKEN_TILE
    fox = w_f is not None
    kind = lambda i: i // n_prompt_tiles
    in_specs = [pl.BlockSpec((tm, d), lambda i: (i, 0)),
                pl.BlockSpec((1, tm, d), lambda i: (layer, kind(i), 0)),
                pl.BlockSpec((1, tm, d), lambda i: (layer, kind(i), 1)),
                pl.BlockSpec(w_main.shape, lambda i: (0, 0))]
    args = [x, mod, mod, w_main]
    out_specs = [pl.BlockSpec((tm, nq), lambda i: (i, 0)),
                 pl.BlockSpec((tm, nkv), lambda i: (i, 0)),
                 pl.BlockSpec((tm, nkv), lambda i: (i, 0)),
                 pl.BlockSpec((tm, nkv), lambda i: (i, 0)),
                 pl.BlockSpec((tm, nkv), lambda i: (i, 0))]
    out_shape = [jax.ShapeDtypeStruct((n, nq), BF16),
                 jax.ShapeDtypeStruct((n, nkv), F32),
                 jax.ShapeDtypeStruct((n, nkv), F32),
                 jax.ShapeDtypeStruct((n, nkv), BF16),
                 jax.ShapeDtypeStruct((n, nkv), BF16)]
    if fox:
        in_specs += [pl.BlockSpec(w_f.shape, lambda i: (0, 0)),
                     pl.BlockSpec(b_f.shape, lambda i: (0, 0))]
        args += [w_f, b_f]
        out_specs.append(pl.BlockSpec((tm, LANES), lambda i: (i, 0)))
        out_shape.append(jax.ShapeDtypeStruct((n, LANES), F32))
    return pl.pallas_call(
        functools.partial(_qkv_body, nq=nq, nkv=nkv, scale=scale, fox=fox),
        grid=(n // tm,),
        in_specs=in_specs, out_specs=out_specs, out_shape=out_shape,
        compiler_params=_params("arbitrary"),
        name="qkv_fox" if fox else "qkv_sb",
    )(*args)


def _stack_heads(q, g, hd):
    return jnp.concatenate([q[:, i * hd:(i + 1) * hd] for i in range(g)], axis=0)


def _unstack_heads(o, g, tq):
    return jnp.concatenate([o[i * tq:(i + 1) * tq] for i in range(g)], axis=1)


def _descending_blocks(jd, carry, block):
    def pair(i, c):
        j = jd - 1 - 2 * i
        return block(j - 1, block(j, c))
    carry = lax.fori_loop(0, jd // 2, pair, carry)
    return lax.cond(jd % 2 == 1, lambda c: block(0, c), lambda c: c, carry)


def _sb_prompt_body(q_ref, k_ref, v_ref, o_ref, *, tq, tk, g, hd):
    qi = pl.program_id(1)
    rows = g * tq
    qs = _stack_heads(q_ref[...], g, hd)
    jd = (qi * tq) // tk
    tri = _tri_incl(tk)

    def block(j, carry, masked=False):
        run, acc = carry
        start = pl.multiple_of(j * tk, tk)
        kb = k_ref[pl.ds(start, tk), :]
        vb = v_ref[pl.ds(start, tk), :]
        z = lax.dot_general(qs, kb, (((1,), (1,)), ((), ())), preferred_element_type=F32)
        sp = _softplus2(z)
        if masked:
            r_io = lax.broadcasted_iota(jnp.int32, (rows, tk), 0)
            c_io = lax.broadcasted_iota(jnp.int32, (rows, tk), 1)
            valid = (j * tk + c_io) < (qi * tq + (r_io & (tq - 1)))
            sp = jnp.where(valid, sp, 0.0)
        c = _cum_dot(sp, tri)
        w = jnp.exp2(z - c - run)
        if masked:
            w = jnp.where(valid, w, 0.0)
        acc = acc + jnp.dot(w.astype(BF16), vb, preferred_element_type=F32)
        return run + c[:, 0:1], acc

    carry = (jnp.zeros((rows, 1), F32), jnp.zeros((rows, hd), F32))
    carry = block(jd, carry, masked=True)
    _, acc = _descending_blocks(jd, carry, block)
    o_ref[...] = _unstack_heads(acc, g, tq).astype(BF16)


def _fox_prompt_body(q_ref, k_ref, v_ref, ck_ref, o_ref, *, tq, tk, g, hd):
    h = pl.program_id(0)
    qi = pl.program_id(1)
    rows = g * tq
    qs = _stack_heads(q_ref[...], g, hd)
    jd = (qi * tq) // tk

    def block(j, carry, masked=False):
        m, l, acc = carry
        start = pl.multiple_of(j * tk, tk)
        kb = k_ref[pl.ds(start, tk), :]
        vb = v_ref[pl.ds(start, tk), :]
        z = lax.dot_general(qs, kb, (((1,), (1,)), ((), ())), preferred_element_type=F32)
        s = jnp.concatenate(
            [z[i * tq:(i + 1) * tq] - ck_ref[pl.ds(h * g + i, 1), pl.ds(start, tk)] for i in range(g)], axis=0)
        if masked:
            r_io = lax.broadcasted_iota(jnp.int32, (rows, tk), 0)
            c_io = lax.broadcasted_iota(jnp.int32, (rows, tk), 1)
            s = jnp.where((j * tk + c_io) <= (qi * tq + (r_io & (tq - 1))), s, NEG_BIG)
        m_new = jnp.maximum(m, jnp.max(s, axis=-1, keepdims=True))
        alpha = jnp.exp2(m - m_new)
        p = jnp.exp2(s - m_new)
        l = alpha * l + jnp.sum(p, axis=-1, keepdims=True)
        acc = alpha * acc + jnp.dot(p.astype(BF16), vb, preferred_element_type=F32)
        return m_new, l, acc

    carry = (jnp.full((rows, 1), NEG_BIG, F32), jnp.zeros((rows, 1), F32), jnp.zeros((rows, hd), F32))
    carry = block(jd, carry, masked=True)
    _, l, acc = _descending_blocks(jd, carry, block)
    o_ref[...] = _unstack_heads(acc / l, g, tq).astype(BF16)


def _prompt_attention(q, kb, vb, ck_t, *, seq, n_kv, g, hd):
    tq, tk = ATT_Q_TILE, ATT_KV_TILE
    fox = ck_t is not None
    in_specs = [pl.BlockSpec((tq, g * hd), lambda h, i: (i, h)),
                pl.BlockSpec((seq, hd), lambda h, i: (0, h)),
                pl.BlockSpec((seq, hd), lambda h, i: (0, h))]
    args = [q, kb, vb]
    if fox:
        in_specs.append(pl.BlockSpec(ck_t.shape, lambda h, i: (0, 0)))
        args.append(ck_t)
    body = _fox_prompt_body if fox else _sb_prompt_body
    return pl.pallas_call(
        functools.partial(body, tq=tq, tk=tk, g=g, hd=hd),
        grid=(n_kv, seq // tq),
        in_specs=in_specs,
        out_specs=pl.BlockSpec((tq, g * hd), lambda h, i: (i, h)),
        out_shape=jax.ShapeDtypeStruct((seq, n_kv * g * hd), BF16),
        compiler_params=_params("arbitrary", "arbitrary"),
        name="fox_prompt_attention" if fox else "sb_prompt_attention",
    )(*args)


def _cumsum_body(lf_ref, o_ref, carry_ref, *, tk):
    @pl.when(pl.program_id(0) == 0)
    def _():
        carry_ref[...] = jnp.zeros_like(carry_ref)
    r = lax.broadcasted_iota(jnp.int32, (tk, tk), 0)
    c = lax.broadcasted_iota(jnp.int32, (tk, tk), 1)
    upper = (r <= c).astype(F32)
    cum = jnp.dot(lf_ref[...], upper, preferred_element_type=F32,
                  precision=lax.Precision.HIGHEST) + carry_ref[:, 0:1]
    o_ref[...] = cum * LOG2E
    carry_ref[...] = jnp.broadcast_to(cum[:, tk - 1:tk], carry_ref.shape)


def _cumsum_lanes(lf_t):
    nh, seq = lf_t.shape
    tk = ATT_KV_TILE
    return pl.pallas_call(
        functools.partial(_cumsum_body, tk=tk),
        grid=(seq // tk,),
        in_specs=[pl.BlockSpec((nh, tk), lambda j: (0, j))],
        out_specs=pl.BlockSpec((nh, tk), lambda j: (0, j)),
        out_shape=jax.ShapeDtypeStruct((nh, seq), F32),
        scratch_shapes=[pltpu.VMEM((nh, LANES), F32)],
        compiler_params=_params("arbitrary"),
        name="logf_cumsum",
    )(lf_t)


def _sample_att_body(pt_ref, q_ref, kn_ref, vn_ref, *rest, n_pages, page, n_kv, g, hd, t_new, fox):
    if fox:
        lfn_ref, kpool, vpool, lfpool, o_ref, kbuf, vbuf, lfbuf, sem = rest
    else:
        kpool, vpool, o_ref, kbuf, vbuf, sem = rest
    b = pl.program_id(0)
    nb = pl.num_programs(0)
    slot = b % 2
    past = n_pages * page
    rows = n_kv * g * t_new

    def copies(bb, sl):
        out = []
        for p in range(n_pages):
            pid = pt_ref[bb, p]
            out.append(pltpu.make_async_copy(kpool.at[pid], kbuf.at[sl, p], sem.at[0, sl]))
            out.append(pltpu.make_async_copy(vpool.at[pid], vbuf.at[sl, p], sem.at[1, sl]))
            if fox:
                out.append(pltpu.make_async_copy(lfpool.at[pid], lfbuf.at[sl, p], sem.at[2, sl]))
        return out

    @pl.when(b == 0)
    def _():
        for cp in copies(0, 0):
            cp.start()

    @pl.when(b + 1 < nb)
    def _():
        for cp in copies(b + 1, 1 - slot):
            cp.start()

    for cp in copies(b, slot):
        cp.wait()

    q = q_ref[0]
    kn = jnp.concatenate([kn_ref[0], jnp.zeros((LANES - t_new, n_kv * hd), F32)], axis=0)
    vn = jnp.concatenate([vn_ref[0], jnp.zeros((LANES - t_new, n_kv * hd), F32)], axis=0)

    z_parts, zn_parts = [], []
    for h in range(n_kv):
        qh = jnp.concatenate([q[:, (h * g + i) * hd:(h * g + i + 1) * hd] for i in range(g)], axis=0)
        kh = kbuf[slot, :, pl.ds(h, page, stride=n_kv), :].reshape(past, hd).astype(BF16)
        z_parts.append(lax.dot_general(qh, kh, (((1,), (1,)), ((), ())), preferred_element_type=F32))
        knh = kn[:, h * hd:(h + 1) * hd].astype(BF16)
        zn_parts.append(lax.dot_general(qh, knh, (((1,), (1,)), ((), ())), preferred_element_type=F32))
    z = jnp.concatenate(z_parts, axis=0)
    zn = jnp.concatenate(zn_parts, axis=0)

    r_io = lax.broadcasted_iota(jnp.int32, (rows, LANES), 0)
    c_io = lax.broadcasted_iota(jnp.int32, (rows, LANES), 1)
    t_row = r_io % t_new

    if not fox:
        valid_n = c_io < t_row
        cn = _cum_dot(jnp.where(valid_n, _softplus2(zn), 0.0), _tri_incl(LANES))
        wn = jnp.where(valid_n, jnp.exp2(zn - cn), 0.0)
        run = cn[:, 0:1]
        tk = ATT_KV_TILE
        tri = _tri_incl(tk)
        w_parts = [None] * (past // tk)
        for j in reversed(range(past // tk)):
            zb = z[:, j * tk:(j + 1) * tk]
            c = _cum_dot(_softplus2(zb), tri)
            w_parts[j] = jnp.exp2(zb - c - run)
            run = run + c[:, 0:1]
        w = jnp.concatenate(w_parts, axis=1)
        denom = None
    else:
        lfn = lfn_ref[0]
        nh = n_kv * g
        tri = _tri_strict(page)
        bias_n = jnp.dot(lfn, tri, preferred_element_type=F32, precision=lax.Precision.HIGHEST)
        carry = jnp.sum(lfn, axis=-1, keepdims=True)
        bias_parts = [None] * n_pages
        for p in reversed(range(n_pages)):
            lfp = lfbuf[slot, p]
            bias_parts[p] = jnp.dot(lfp, tri, preferred_element_type=F32,
                                    precision=lax.Precision.HIGHEST) + carry
            carry = carry + jnp.sum(lfp, axis=-1, keepdims=True)
        bias_h = jnp.concatenate(bias_parts, axis=1)
        bias = jnp.concatenate([jnp.broadcast_to(bias_h[i:i + 1], (t_new, past)) for i in range(nh)], axis=0)
        bias_new = jnp.concatenate([jnp.broadcast_to(bias_n[i:i + 1], (t_new, LANES)) for i in range(nh)], axis=0)
        s = z + bias * LOG2E
        sn = jnp.where(c_io <= t_row, zn + bias_new * LOG2E, NEG_BIG)
        m = jnp.maximum(jnp.max(s, axis=-1, keepdims=True), jnp.max(sn, axis=-1, keepdims=True))
        w = jnp.exp2(s - m)
        wn = jnp.exp2(sn - m)
        denom = jnp.sum(w, axis=-1, keepdims=True) + jnp.sum(wn, axis=-1, keepdims=True)

    wb = w.astype(BF16)
    wnb = wn.astype(BF16)
    o_cols = []
    rh = g * t_new
    for h in range(n_kv):
        vh = vbuf[slot, :, pl.ds(h, page, stride=n_kv), :].reshape(past, hd).astype(BF16)
        vnh = vn[:, h * hd:(h + 1) * hd].astype(BF16)
        oh = (jnp.dot(wb[h * rh:(h + 1) * rh], vh, preferred_element_type=F32)
              + jnp.dot(wnb[h * rh:(h + 1) * rh], vnh, preferred_element_type=F32))
        if fox:
            oh = oh / denom[h * rh:(h + 1) * rh]
        for i in range(g):
            o_cols.append(oh[i * t_new:(i + 1) * t_new])
    o_ref[0] = jnp.concatenate(o_cols, axis=1).astype(BF16)


def _sample_attention(page_table, q_b, k_new, v_new, k_pool, v_pool, lf_new_t, lf_pool_t, *, n_kv, g, hd):
    nb, t_new, dq = q_b.shape
    n_pages = page_table.shape[1]
    page = k_pool.shape[1] // n_kv
    nkv = n_kv * hd
    fox = lf_pool_t is not None
    in_specs = [pl.BlockSpec((1, t_new, dq), lambda b, pt: (b, 0, 0)),
                pl.BlockSpec((1, t_new, nkv), lambda b, pt: (b, 0, 0)),
                pl.BlockSpec((1, t_new, nkv), lambda b, pt: (b, 0, 0))]
    args = [q_b, k_new, v_new]
    scratch = [pltpu.VMEM((2, n_pages, page * n_kv, hd), F32),
               pltpu.VMEM((2, n_pages, page * n_kv, hd), F32)]
    if fox:
        nh = n_kv * g
        in_specs.append(pl.BlockSpec((1, nh, LANES), lambda b, pt: (b, 0, 0)))
        args.append(lf_new_t)
    in_specs += [pl.BlockSpec(memory_space=pl.ANY), pl.BlockSpec(memory_space=pl.ANY)]
    args += [k_pool, v_pool]
    if fox:
        in_specs.append(pl.BlockSpec(memory_space=pl.ANY))
        args.append(lf_pool_t)
        scratch.append(pltpu.VMEM((2, n_pages, nh, page), F32))
    scratch.append(pltpu.SemaphoreType.DMA((3, 2)))
    return pl.pallas_call(
        functools.partial(_sample_att_body, n_pages=n_pages, page=page, n_kv=n_kv, g=g, hd=hd,
                          t_new=t_new, fox=fox),
        grid_spec=pltpu.PrefetchScalarGridSpec(
            num_scalar_prefetch=1, grid=(nb,), in_specs=in_specs,
            out_specs=pl.BlockSpec((1, t_new, dq), lambda b, pt: (b, 0, 0)),
            scratch_shapes=scratch),
        out_shape=jax.ShapeDtypeStruct((nb, t_new, dq), BF16),
        compiler_params=_params("arbitrary"),
        name="fox_sample_attention" if fox else "sb_sample_attention",
    )(page_table, *args)


def _layernorm(u, gamma, beta):
    mu = jnp.mean(u, axis=-1, keepdims=True)
    uc = u - mu
    var = jnp.mean(uc * uc, axis=-1, keepdims=True)
    return uc * lax.rsqrt(var + LN_EPS) * gamma + beta


def _first_index_of_max(vals, mask, lane):
    top = jnp.max(jnp.where(mask, vals, NEG_BIG), axis=-1, keepdims=True)
    idx = jnp.min(jnp.where(mask & (vals == top), lane, 2 * LANES), axis=-1, keepdims=True)
    return top, idx


def _post_attn_body(o_ref, x_ref, g1_ref, sh2_ref, sc2_ref, w_ref, lng_ref, lnb_ref, rw_ref, rb_ref,
                    x1_ref, h2_ref, eid_ref, gate_ref, *, alpha, n_exp):
    y = jnp.dot(o_ref[...], w_ref[...], preferred_element_type=F32)
    x1 = _layernorm(alpha * x_ref[...] + g1_ref[0] * y, lng_ref[...], lnb_ref[...])
    x1_ref[...] = x1
    h2 = x1 * (1.0 + sc2_ref[0]) + sh2_ref[0]
    h2_ref[...] = h2

    logits = _dot_split(h2, rw_ref[...])
    lane = lax.broadcasted_iota(jnp.int32, logits.shape, 1)
    real = lane < n_exp
    logits = jnp.where(real, logits, NEG_BIG)
    e = jnp.exp(logits - jnp.max(logits, axis=-1, keepdims=True))
    probs = e / jnp.sum(e, axis=-1, keepdims=True)
    sel = probs + rb_ref[...]
    epg = n_exp // N_GROUPS
    best = first = second = None
    for grp in range(N_GROUPS):
        in_g = (lane >= grp * epg) & (lane < (grp + 1) * epg)
        top1, idx1 = _first_index_of_max(sel, in_g, lane)
        top2, idx2 = _first_index_of_max(sel, in_g & (lane != idx1), lane)
        score = top1 + top2
        if grp == 0:
            best, first, second = score, idx1, idx2
        else:
            better = score > best
            best = jnp.where(better, score, best)
            first = jnp.where(better, idx1, first)
            second = jnp.where(better, idx2, second)
    gate1 = jnp.sum(jnp.where(lane == first, probs, 0.0), axis=-1, keepdims=True)
    gate2 = jnp.sum(jnp.where(lane == second, probs, 0.0), axis=-1, keepdims=True)
    tot = gate1 + gate2
    eid_ref[...] = jnp.where(lane == 0, first, jnp.where(lane == 1, second, 0))
    gate_ref[...] = jnp.where(lane == 0, gate1 / tot, jnp.where(lane == 1, gate2 / tot, 0.0))


def _post_attention(o, x, mod, layer, w_out, ln_g, ln_b, rw_pad, rb_pad, *, n_prompt_tiles, alpha, n_exp):
    n, d = x.shape
    tm = TOKEN_TILE
    kind = lambda i: i // n_prompt_tiles
    row = pl.BlockSpec((tm, d), lambda i: (i, 0))
    modspec = lambda c: pl.BlockSpec((1, tm, d), lambda i: (layer, kind(i), c))
    const = lambda a: pl.BlockSpec(a.shape, lambda i: (0,) * a.ndim)
    lanes = pl.BlockSpec((tm, LANES), lambda i: (i, 0))
    return pl.pallas_call(
        functools.partial(_post_attn_body, alpha=alpha, n_exp=n_exp),
        grid=(n // tm,),
        in_specs=[row, row, modspec(2), modspec(3), modspec(4), const(w_out), const(ln_g), const(ln_b),
                  const(rw_pad), const(rb_pad)],
        out_specs=[row, row, lanes, lanes],
        out_shape=[jax.ShapeDtypeStruct((n, d), F32), jax.ShapeDtypeStruct((n, d), F32),
                   jax.ShapeDtypeStruct((n, LANES), jnp.int32), jax.ShapeDtypeStruct((n, LANES), F32)],
        compiler_params=_params("arbitrary"),
        name="post_attention",
    )(o, x, mod, mod, mod, w_out, ln_g, ln_b, rw_pad, rb_pad)


def _moe_body(texp_ref, trows_ref, src_ref, dst_ref, srcn_ref, gate_ref, wg_ref, wu_ref, wd_ref,
              h_hbm, out_hbm, xbuf, ybuf, sem, *, tm):
    t = pl.program_id(0)
    nt = pl.num_programs(0)
    slot = t % 2
    del texp_ref
    rows_of = lambda i: trows_ref[jnp.clip(i, 0, nt - 1)]

    def start_gather(idx_ref, sl):
        def chunk(c, _):
            for u in range(ROW_DMA_UNROLL):
                r = c * ROW_DMA_UNROLL + u
                pltpu.make_async_copy(h_hbm.at[idx_ref[0, 0, r]], xbuf.at[sl, r], sem.at[0, sl]).start()
            return 0
        lax.fori_loop(0, tm // ROW_DMA_UNROLL, chunk, 0)

    def wait_gather(sl):
        pltpu.make_async_copy(h_hbm.at[pl.ds(0, tm)], xbuf.at[sl], sem.at[0, sl]).wait()

    def start_scatter(sl, n_rows):
        def one(r):
            pltpu.make_async_copy(ybuf.at[sl, r], out_hbm.at[dst_ref[0, 0, r]], sem.at[1, sl]).start()

        def chunk(c, _):
            for u in range(ROW_DMA_UNROLL):
                one(c * ROW_DMA_UNROLL + u)
            return 0
        n_full = n_rows // ROW_DMA_UNROLL
        lax.fori_loop(0, n_full, chunk, 0)
        lax.fori_loop(n_full * ROW_DMA_UNROLL, n_rows, lambda r, _: (one(r), 0)[1], 0)

    def wait_scatter(sl, n_rows):
        p = tm
        while p >= 1:
            @pl.when((n_rows & p) != 0)
            def _(p=p):
                pltpu.make_async_copy(ybuf.at[sl, pl.ds(0, p)], out_hbm.at[pl.ds(0, p)], sem.at[1, sl]).wait()
            p //= 2

    @pl.when((t == 0) & (rows_of(0) > 0))
    def _():
        start_gather(src_ref, 0)

    @pl.when((t + 1 < nt) & (rows_of(t + 1) > 0))
    def _():
        start_gather(srcn_ref, 1 - slot)

    @pl.when(t >= 2)
    def _():
        wait_scatter(slot, rows_of(t - 2))

    @pl.when(rows_of(t) > 0)
    def _():
        wait_gather(slot)
        xb = xbuf[slot].astype(BF16)
        gp = jnp.dot(xb, wg_ref[0, 0].astype(BF16), preferred_element_type=F32)
        up = jnp.dot(xb, wu_ref[0, 0].astype(BF16), preferred_element_type=F32)
        act = gp / (1.0 + jnp.exp(-gp)) * up * gate_ref[...]
        ybuf[slot] = jnp.dot(act.astype(BF16), wd_ref[0, 0].astype(BF16), preferred_element_type=F32)
        start_scatter(slot, rows_of(t))

    @pl.when(t == nt - 1)
    def _():
        @pl.when(nt >= 2)
        def _():
            wait_scatter(1 - slot, rows_of(t - 1))
        wait_scatter(slot, rows_of(t))


def _moe(h2, tile_expert, tile_rows, row_src, row_dst, row_gate, w_gate, w_up, w_down, layer):
    n, d = h2.shape
    tm = MOE_ROW_TILE
    n_tiles = tile_expert.shape[0]
    f = w_gate.shape[-1]
    nxt = lambda t: jnp.minimum(t + 1, n_tiles - 1)
    smem_rows = lambda imap: pl.BlockSpec((1, 1, tm), imap, memory_space=pltpu.SMEM)
    return pl.pallas_call(
        functools.partial(_moe_body, tm=tm),
        grid_spec=pltpu.PrefetchScalarGridSpec(
            num_scalar_prefetch=2, grid=(n_tiles,),
            in_specs=[smem_rows(lambda t, te, tr: (t, 0, 0)),
                      smem_rows(lambda t, te, tr: (t, 0, 0)),
                      smem_rows(lambda t, te, tr: (nxt(t), 0, 0)),
                      pl.BlockSpec((tm, 1), lambda t, te, tr: (t, 0)),
                      pl.BlockSpec((1, 1, d, f), lambda t, te, tr: (layer, te[t], 0, 0)),
                      pl.BlockSpec((1, 1, d, f), lambda t, te, tr: (layer, te[t], 0, 0)),
                      pl.BlockSpec((1, 1, f, d), lambda t, te, tr: (layer, te[t], 0, 0)),
                      pl.BlockSpec(memory_space=pl.ANY)],
            out_specs=pl.BlockSpec(memory_space=pl.ANY),
            scratch_shapes=[pltpu.VMEM((2, tm, d), F32), pltpu.VMEM((2, tm, d), F32),
                            pltpu.SemaphoreType.DMA((2, 2))]),
        out_shape=jax.ShapeDtypeStruct((TOP_K * n, d), F32),
        compiler_params=_params("arbitrary"),
        name="sparse_experts",
    )(tile_expert, tile_rows, row_src, row_dst, row_src, row_gate, w_gate, w_up, w_down, h2)


def _routing_tables(eid, gate, n_exp, tm):
    n = eid.shape[0]
    n_assign = n * TOP_K
    n_tiles = (n_assign + n_exp * (tm - 1)) // tm + 1
    e_flat = eid.reshape(n_assign)
    order = jnp.argsort(e_flat, stable=True).astype(jnp.int32)
    counts = jnp.sum((e_flat[:, None] == jnp.arange(n_exp, dtype=jnp.int32)[None, :]).astype(jnp.int32), axis=0)
    starts = jnp.cumsum(counts) - counts
    tiles_per = (counts + tm - 1) // tm
    tile_starts = jnp.cumsum(tiles_per) - tiles_per
    n_used = jnp.sum(tiles_per).astype(jnp.int32)
    tile_ids = jnp.arange(n_tiles, dtype=jnp.int32)
    tile_expert = jnp.clip(jnp.searchsorted(tile_starts + tiles_per, tile_ids, side="right"), 0, n_exp - 1)
    tile_expert = tile_expert.astype(jnp.int32)
    row_in_tile = jnp.arange(tm, dtype=jnp.int32)[None, :]
    within = (tile_ids - tile_starts[tile_expert])[:, None] * tm + row_in_tile
    valid = (within < counts[tile_expert][:, None]) & (tile_ids[:, None] < n_used)
    sorted_pos = jnp.clip(starts[tile_expert][:, None] + within, 0, n_assign - 1)
    assign = order[sorted_pos]
    tok = assign // TOP_K
    slot_k = assign % TOP_K
    row_src = jnp.where(valid, tok, 0).astype(jnp.int32)
    row_dst = jnp.where(valid, slot_k * n + tok, 0).astype(jnp.int32)
    row_gate = jnp.where(valid, gate.reshape(n_assign)[assign], 0.0).astype(F32)
    tile_rows = jnp.sum(valid.astype(jnp.int32), axis=1)
    return (tile_expert, tile_rows, row_src.reshape(n_tiles, 1, tm),
            row_dst.reshape(n_tiles, 1, tm), row_gate.reshape(n_tiles * tm, 1))


def _combine_body(x1_ref, ya_ref, yb_ref, g2_ref, lng_ref, lnb_ref, o_ref, *, alpha):
    y = ya_ref[...] + yb_ref[...]
    o_ref[...] = _layernorm(alpha * x1_ref[...] + g2_ref[0] * y, lng_ref[...], lnb_ref[...])


def _combine(x1, y2, mod, layer, ln_g, ln_b, *, n_prompt_tiles, alpha):
    n, d = x1.shape
    tm = TOKEN_TILE
    kind = lambda i: i // n_prompt_tiles
    row = pl.BlockSpec((tm, d), lambda i: (i, 0))
    const = lambda a: pl.BlockSpec(a.shape, lambda i: (0,) * a.ndim)
    return pl.pallas_call(
        functools.partial(_combine_body, alpha=alpha),
        grid=(n // tm,),
        in_specs=[row, row, pl.BlockSpec((tm, d), lambda i: (n // tm + i, 0)),
                  pl.BlockSpec((1, tm, d), lambda i: (layer, kind(i), 5)), const(ln_g), const(ln_b)],
        out_specs=row,
        out_shape=jax.ShapeDtypeStruct((n, d), F32),
        compiler_params=_params("arbitrary"),
        name="combine_layernorm",
    )(x1, y2, y2, mod, ln_g, ln_b)


def kernel(x_prompt, x_sample, cache_k_sb, cache_v_sb, cache_k_fox, cache_v_fox, cache_logf_fox, page_table,
           c_prompt, c_sample, ada_w, ada_b, ln_g, ln_b, w_in_sb, w_out_sb, w_in_fox, b_f_fox, w_out_fox,
           router_w, router_b, w_gate, w_up, w_down):
    batch, seq, d = x_prompt.shape
    nb, t_new, _ = x_sample.shape
    depth = ada_w.shape[0]
    n_pool, page, n_kv, hd = cache_k_sb.shape[1:]
    n_heads = d // hd
    g = n_heads // n_kv
    nq, nkv = n_heads * hd, n_kv * hd
    n_exp = router_w.shape[1]
    scale = float(hd) ** -0.5 * LOG2E
    alpha = (2.0 * depth) ** 0.25
    tm = TOKEN_TILE
    n_p, n_s = batch * seq, nb * t_new
    n = n_p + n_s
    assert batch == 1 and n_p % tm == 0 and n_s % tm == 0 and tm % nb == 0
    assert seq % ATT_KV_TILE == 0 and (page * page_table.shape[1]) % ATT_KV_TILE == 0
    assert n_exp % N_GROUPS == 0 and n_exp <= LANES and n_heads <= LANES and t_new <= LANES
    n_prompt_tiles = n_p // tm

    xs_t = jnp.transpose(x_sample, (1, 0, 2)).reshape(n_s, d)
    x = jnp.concatenate([x_prompt.reshape(n_p, d), xs_t], axis=0)
    c_rows = jnp.concatenate([jnp.broadcast_to(c_prompt, (tm, d))] + [c_sample] * (tm // nb), axis=0)
    mod = _modulation(c_rows, ada_w, ada_b)

    rw_pad = _hi_lo_columns(router_w)
    rb_pad = jnp.pad(router_b.reshape(1, n_exp), ((0, 0), (0, LANES - n_exp)))

    def to_seq_major(a):
        return jnp.transpose(a.reshape(t_new, nb, a.shape[-1]), (1, 0, 2))

    def to_pos_major(a):
        return jnp.transpose(a, (1, 0, 2)).reshape(n_s, a.shape[-1])

    sb_out, fox_out = [], []
    for layer in range(depth):
        j = layer // 2
        fox = layer % 2 == 1
        if fox:
            w_in, w_out = w_in_fox[j], w_out_fox[j]
            w_main = w_in[:, :nq + 2 * nkv].astype(BF16)
            w_f = _hi_lo_columns(w_in[:, nq + 2 * nkv:])
            b_f = jnp.pad(b_f_fox[j].reshape(1, n_heads), ((0, 0), (0, LANES - n_heads)))
            k_pool, v_pool = cache_k_fox[j], cache_v_fox[j]
        else:
            w_in, w_out = w_in_sb[j], w_out_sb[j]
            w_main, w_f, b_f = w_in.astype(BF16), None, None
            k_pool, v_pool = cache_k_sb[j], cache_v_sb[j]
        res = _qkv_proj(x, mod, layer, w_main, w_f, b_f, n_prompt_tiles=n_prompt_tiles,
                        nq=nq, nkv=nkv, scale=scale)
        q, k, v, kb, vb = res[:5]
        k_pool = k_pool.reshape(n_pool, page * n_kv, hd)
        v_pool = v_pool.reshape(n_pool, page * n_kv, hd)
        q_s = to_seq_major(q[n_p:])
        k_s, v_s = to_seq_major(k[n_p:]), to_seq_major(v[n_p:])
        if fox:
            lf = res[5][:, :n_heads]
            ck_t = _cumsum_lanes(jnp.transpose(lf[:n_p]))
            lf_s = to_seq_major(lf[n_p:])
            lf_new_t = jnp.pad(jnp.transpose(lf_s, (0, 2, 1)), ((0, 0), (0, 0), (0, LANES - t_new)))
            lf_pool_t = jnp.transpose(cache_logf_fox[j], (0, 2, 1))
            o_p = _prompt_attention(q, kb, vb, ck_t, seq=seq, n_kv=n_kv, g=g, hd=hd)
            o_s = _sample_attention(page_table, q_s, k_s, v_s, k_pool, v_pool, lf_new_t, lf_pool_t,
                                    n_kv=n_kv, g=g, hd=hd)
            fox_out.append((k[:n_p].reshape(batch, seq, n_kv, hd), v[:n_p].reshape(batch, seq, n_kv, hd),
                            lf[:n_p].reshape(batch, seq, n_heads),
                            k_s.reshape(nb, t_new, n_kv, hd), v_s.reshape(nb, t_new, n_kv, hd), lf_s))
        else:
            o_p = _prompt_attention(q, kb, vb, None, seq=seq, n_kv=n_kv, g=g, hd=hd)
            o_s = _sample_attention(page_table, q_s, k_s, v_s, k_pool, v_pool, None, None,
                                    n_kv=n_kv, g=g, hd=hd)
            sb_out.append((k[:n_p].reshape(batch, seq, n_kv, hd), v[:n_p].reshape(batch, seq, n_kv, hd),
                           k_s.reshape(nb, t_new, n_kv, hd), v_s.reshape(nb, t_new, n_kv, hd)))
        o = jnp.concatenate([o_p, to_pos_major(o_s)], axis=0)
        x1, h2, eid, gate = _post_attention(
            o, x, mod, layer, w_out.astype(BF16), ln_g[layer, 0].reshape(1, d), ln_b[layer, 0].reshape(1, d),
            rw_pad, rb_pad, n_prompt_tiles=n_prompt_tiles, alpha=alpha, n_exp=n_exp)
        tables = _routing_tables(eid[:, :TOP_K], gate[:, :TOP_K], n_exp, MOE_ROW_TILE)
        y2 = _moe(h2, *tables, w_gate, w_up, w_down, layer)
        x = _combine(x1, y2, mod, layer, ln_g[layer, 1].reshape(1, d), ln_b[layer, 1].reshape(1, d),
                     n_prompt_tiles=n_prompt_tiles, alpha=alpha)

    y_prompt = x[:n_p].reshape(batch, seq, d)
    y_sample = to_seq_major(x[n_p:])
    stack = lambda items, i: jnp.stack([e[i] for e in items])
    return (y_prompt, y_sample,
            stack(sb_out, 0), stack(sb_out, 1), stack(sb_out, 2), stack(sb_out, 3),
            stack(fox_out, 0), stack(fox_out, 1), stack(fox_out, 2),
            stack(fox_out, 3), stack(fox_out, 4), stack(fox_out, 5))
```

```python
import functools

import jax
import jax.numpy as jnp
from jax import lax
from jax.experimental import pallas as pl
from jax.experimental.pallas import tpu as pltpu

F32 = jnp.float32
BF16 = jnp.bfloat16

N_GROUPS = 4
TOP_K = 2
LN_EPS = 1e-5
NEG_BIG = -1e30

TOKEN_TILE = 256
MOE_ROW_TILE = 256
ROW_DMA_UNROLL = 8
ATT_Q_TILE = 256
CUMSUM_TILE = 256
ATT_KV_TILE = 512
FOX_KV_TILE = 512
LANES = 128
V7X_VMEM_LIMIT = 56 * 1024 * 1024


def _params(*sem, flags=None):
    return pltpu.CompilerParams(dimension_semantics=sem, vmem_limit_bytes=V7X_VMEM_LIMIT, flags=flags)


LOG2E = 1.4426950408889634


def _log_sigmoid_neg(z):
    return -(jnp.maximum(z, 0.0) + jnp.log(1.0 + jnp.exp(-jnp.abs(z))))


def _softplus2(z2):
    neg_abs = pltpu.bitcast(pltpu.bitcast(z2, jnp.uint32) | jnp.uint32(0x80000000), F32)
    return jnp.maximum(z2, 0.0) + jnp.log2(1.0 + jnp.exp2(neg_abs))


def _cum_dot(a, tri):
    return jnp.dot(a.astype(BF16), tri, preferred_element_type=F32)


def _hi_lo_columns(w):
    w = jnp.pad(w, ((0, 0), (0, LANES - w.shape[1])))
    hi = w.astype(BF16)
    lo = (w - hi.astype(F32)).astype(BF16)
    return jnp.concatenate([hi, lo], axis=1)


def _dot_split(a, w_hl):
    hi = a.astype(BF16)
    lo = (a - hi.astype(F32)).astype(BF16)
    p = jnp.dot(hi, w_hl, preferred_element_type=F32)
    q = jnp.dot(lo, w_hl[:, :LANES], preferred_element_type=F32)
    return p[:, :LANES] + p[:, LANES:] + q


def _tri_incl(n):
    r = lax.broadcasted_iota(jnp.int32, (n, n), 0)
    c = lax.broadcasted_iota(jnp.int32, (n, n), 1)
    return (r >= c).astype(BF16)


def _tri_strict(n):
    r = lax.broadcasted_iota(jnp.int32, (n, n), 0)
    c = lax.broadcasted_iota(jnp.int32, (n, n), 1)
    return (r > c).astype(F32)


def _mod_body(c_ref, w_ref, b_ref, o_ref):
    c = c_ref[...]
    a = c / (1.0 + jnp.exp(-c))
    o_ref[0] = jnp.dot(a.astype(BF16), w_ref[0].astype(BF16),
                       preferred_element_type=F32) + b_ref[0]


def _modulation(c_rows, ada_w, ada_b):
    depth, d, d6 = ada_w.shape
    rows = c_rows.shape[0]
    tn = min(1024, d6)
    return pl.pallas_call(
        _mod_body,
        grid=(depth, d6 // tn),
        in_specs=[pl.BlockSpec((rows, d), lambda l, j: (0, 0)),
                  pl.BlockSpec((1, d, tn), lambda l, j: (l, 0, j)),
                  pl.BlockSpec((1, 1, tn), lambda l, j: (l, 0, j))],
        out_specs=pl.BlockSpec((1, rows, tn), lambda l, j: (l, 0, j)),
        out_shape=jax.ShapeDtypeStruct((depth, rows, d6), F32),
        compiler_params=_params("arbitrary", "arbitrary"),
        name="modulation",
    )(c_rows, ada_w, ada_b.reshape(depth, 1, d6))


def _qkv_body(*refs, nq, nkv, scale, fox, n_prompt_tiles, nb):
    if fox:
        (x_ref, sh_ref, sc_ref, w_ref, wf_ref, bf_ref, qp_ref, kp_ref, vp_ref, kbp_ref, vbp_ref,
         qs_ref, ks_ref, vs_ref, lfp_ref, lfs_ref) = refs
    else:
        (x_ref, sh_ref, sc_ref, w_ref, qp_ref, kp_ref, vp_ref, kbp_ref, vbp_ref,
         qs_ref, ks_ref, vs_ref) = refs
    i = pl.program_id(0)
    h = x_ref[...] * (1.0 + sc_ref[0]) + sh_ref[0]
    p = jnp.dot(h.astype(BF16), w_ref[...], preferred_element_type=F32)
    q = (p[:, :nq] * scale).astype(BF16)
    k = p[:, nq:nq + nkv]
    v = p[:, nq + nkv:nq + 2 * nkv]
    if fox:
        lf = _log_sigmoid_neg(-(_dot_split(h, wf_ref[...]) + bf_ref[...]))

    @pl.when(i < n_prompt_tiles)
    def _():
        qp_ref[...] = q
        kp_ref[...] = k
        vp_ref[...] = v
        kbp_ref[...] = k.astype(BF16)
        vbp_ref[...] = v.astype(BF16)
        if fox:
            lfp_ref[...] = lf

    @pl.when(i >= n_prompt_tiles)
    def _():
        for t in range(q.shape[0] // nb):
            rows = slice(t * nb, (t + 1) * nb)
            qs_ref[:, t * nq:(t + 1) * nq] = q[rows]
            ks_ref[:, t * nkv:(t + 1) * nkv] = k[rows]
            vs_ref[:, t * nkv:(t + 1) * nkv] = v[rows]
            if fox:
                lfs_ref[:, t * LANES:(t + 1) * LANES] = lf[rows]


def _qkv_proj(x, mod, layer, w_main, w_f, b_f, *, n_prompt_tiles, nb, nq, nkv, scale):
    n, d = x.shape
    tm = TOKEN_TILE
    fox = w_f is not None
    n_p = n_prompt_tiles * tm
    t_new = (n - n_p) // nb
    per = tm // nb
    kind = lambda i: i // n_prompt_tiles
    in_specs = [pl.BlockSpec((tm, d), lambda i: (i, 0)),
                pl.BlockSpec((1, tm, d), lambda i: (layer, kind(i), 0)),
                pl.BlockSpec((1, tm, d), lambda i: (layer, kind(i), 1)),
                pl.BlockSpec(w_main.shape, lambda i: (0, 0))]
    args = [x, mod, mod, w_main]
    if fox:
        in_specs += [pl.BlockSpec(w_f.shape, lambda i: (0, 0)),
                     pl.BlockSpec(b_f.shape, lambda i: (0, 0))]
        args += [w_f, b_f]
    p_spec = lambda w: pl.BlockSpec((tm, w), lambda i: (jnp.minimum(i, n_prompt_tiles - 1), 0))
    s_spec = lambda w: pl.BlockSpec((nb, per * w), lambda i: (0, jnp.maximum(i - n_prompt_tiles, 0)))
    p_shape = lambda w, dt: jax.ShapeDtypeStruct((n_p, w), dt)
    s_shape = lambda w, dt: jax.ShapeDtypeStruct((nb, t_new * w), dt)
    out_specs = [p_spec(nq), p_spec(nkv), p_spec(nkv), p_spec(nkv), p_spec(nkv),
                 s_spec(nq), s_spec(nkv), s_spec(nkv)]
    out_shape = [p_shape(nq, BF16), p_shape(nkv, F32), p_shape(nkv, F32), p_shape(nkv, BF16), p_shape(nkv, BF16),
                 s_shape(nq, BF16), s_shape(nkv, F32), s_shape(nkv, F32)]
    if fox:
        out_specs += [p_spec(LANES), s_spec(LANES)]
        out_shape += [p_shape(LANES, F32), s_shape(LANES, F32)]
    return pl.pallas_call(
        functools.partial(_qkv_body, nq=nq, nkv=nkv, scale=scale, fox=fox,
                          n_prompt_tiles=n_prompt_tiles, nb=nb),
        grid=(n // tm,),
        in_specs=in_specs, out_specs=out_specs, out_shape=out_shape,
        compiler_params=_params("arbitrary"),
        name="qkv_fox" if fox else "qkv_sb",
    )(*args)


def _stack_heads(q, g, hd):
    return jnp.concatenate([q[:, i * hd:(i + 1) * hd] for i in range(g)], axis=0)


def _unstack_heads(o, g, tq):
    return jnp.concatenate([o[i * tq:(i + 1) * tq] for i in range(g)], axis=1)


def _descending_blocks(jd, carry, block):
    def pair(i, c):
        j = jd - 1 - 2 * i
        return block(j - 1, block(j, c))
    carry = lax.fori_loop(0, jd // 2, pair, carry)
    return lax.cond(jd % 2 == 1, lambda c: block(0, c), lambda c: c, carry)


def _sb_prompt_body(q_ref, k_ref, v_ref, o_ref, *, tq, tk, g, hd):
    qi = pl.program_id(1)
    rows = g * tq
    qs = _stack_heads(q_ref[...], g, hd)
    jd = (qi * tq) // tk
    tc = CUMSUM_TILE
    tri = _tri_incl(tc)

    def block(j, carry, masked=False):
        run, acc = carry
        start = pl.multiple_of(j * tk, tk)
        kb = k_ref[pl.ds(start, tk), :]
        vb = v_ref[pl.ds(start, tk), :]
        z = lax.dot_general(qs, kb, (((1,), (1,)), ((), ())), preferred_element_type=F32)
        sp = _softplus2(z)
        if masked:
            r_io = lax.broadcasted_iota(jnp.int32, (rows, tk), 0)
            c_io = lax.broadcasted_iota(jnp.int32, (rows, tk), 1)
            valid = (j * tk + c_io) < (qi * tq + (r_io & (tq - 1)))
            sp = jnp.where(valid, sp, 0.0)
        w_parts = [None] * (tk // tc)
        for i in reversed(range(tk // tc)):
            c = _cum_dot(sp[:, i * tc:(i + 1) * tc], tri)
            w_parts[i] = jnp.exp2(z[:, i * tc:(i + 1) * tc] - c - run)
            run = run + c[:, 0:1]
        w = jnp.concatenate(w_parts, axis=1)
        if masked:
            w = jnp.where(valid, w, 0.0)
        acc = acc + jnp.dot(w.astype(BF16), vb, preferred_element_type=F32)
        return run, acc

    carry = (jnp.zeros((rows, 1), F32), jnp.zeros((rows, hd), F32))
    carry = block(jd, carry, masked=True)
    _, acc = _descending_blocks(jd, carry, block)
    o_ref[...] = _unstack_heads(acc, g, tq).astype(BF16)


def _fox_prompt_body(q_ref, k_ref, v_ref, ck_ref, o_ref, *, tq, tk, g, hd):
    h = pl.program_id(0)
    qi = pl.program_id(1)
    rows = g * tq
    qs = _stack_heads(q_ref[...], g, hd)
    jd = (qi * tq) // tk

    def block(j, carry, masked=False):
        m, l, acc = carry
        start = pl.multiple_of(j * tk, tk)
        kb = k_ref[pl.ds(start, tk), :]
        vb = v_ref[pl.ds(start, tk), :]
        z = lax.dot_general(qs, kb, (((1,), (1,)), ((), ())), preferred_element_type=F32)
        s = jnp.concatenate(
            [z[i * tq:(i + 1) * tq] - ck_ref[pl.ds(h * g + i, 1), pl.ds(start, tk)] for i in range(g)], axis=0)
        if masked:
            r_io = lax.broadcasted_iota(jnp.int32, (rows, tk), 0)
            c_io = lax.broadcasted_iota(jnp.int32, (rows, tk), 1)
            s = jnp.where((j * tk + c_io) <= (qi * tq + (r_io & (tq - 1))), s, NEG_BIG)
        m_new = jnp.maximum(m, jnp.max(s, axis=-1, keepdims=True))
        alpha = jnp.exp2(m - m_new)
        p = jnp.exp2(s - m_new)
        l = alpha * l + jnp.sum(p, axis=-1, keepdims=True)
        acc = alpha * acc + jnp.dot(p.astype(BF16), vb, preferred_element_type=F32)
        return m_new, l, acc

    carry = (jnp.full((rows, 1), NEG_BIG, F32), jnp.zeros((rows, 1), F32), jnp.zeros((rows, hd), F32))
    carry = block(jd, carry, masked=True)
    _, l, acc = _descending_blocks(jd, carry, block)
    o_ref[...] = _unstack_heads(acc / l, g, tq).astype(BF16)


def _prompt_attention(q, kb, vb, ck_t, *, seq, n_kv, g, hd):
    fox = ck_t is not None
    tq, tk = ATT_Q_TILE, (FOX_KV_TILE if fox else ATT_KV_TILE)
    in_specs = [pl.BlockSpec((tq, g * hd), lambda h, i: (i, h)),
                pl.BlockSpec((seq, hd), lambda h, i: (0, h)),
                pl.BlockSpec((seq, hd), lambda h, i: (0, h))]
    args = [q, kb, vb]
    if fox:
        in_specs.append(pl.BlockSpec(ck_t.shape, lambda h, i: (0, 0)))
        args.append(ck_t)
    body = _fox_prompt_body if fox else _sb_prompt_body
    return pl.pallas_call(
        functools.partial(body, tq=tq, tk=tk, g=g, hd=hd),
        grid=(n_kv, seq // tq),
        in_specs=in_specs,
        out_specs=pl.BlockSpec((tq, g * hd), lambda h, i: (i, h)),
        out_shape=jax.ShapeDtypeStruct((seq, n_kv * g * hd), BF16),
        compiler_params=_params("arbitrary", "arbitrary"),
        name="fox_prompt_attention" if fox else "sb_prompt_attention",
    )(*args)


def _cumsum_body(lf_ref, o_ref, carry_ref, *, tk):
    @pl.when(pl.program_id(0) == 0)
    def _():
        carry_ref[...] = jnp.zeros_like(carry_ref)
    r = lax.broadcasted_iota(jnp.int32, (tk, tk), 0)
    c = lax.broadcasted_iota(jnp.int32, (tk, tk), 1)
    upper = (r <= c).astype(F32)
    cum = jnp.dot(lf_ref[...], upper, preferred_element_type=F32,
                  precision=lax.Precision.HIGHEST) + carry_ref[:, 0:1]
    o_ref[...] = cum * LOG2E
    carry_ref[...] = jnp.broadcast_to(cum[:, tk - 1:tk], carry_ref.shape)


def _cumsum_lanes(lf_t):
    nh, seq = lf_t.shape
    tk = CUMSUM_TILE
    return pl.pallas_call(
        functools.partial(_cumsum_body, tk=tk),
        grid=(seq // tk,),
        in_specs=[pl.BlockSpec((nh, tk), lambda j: (0, j))],
        out_specs=pl.BlockSpec((nh, tk), lambda j: (0, j)),
        out_shape=jax.ShapeDtypeStruct((nh, seq), F32),
        scratch_shapes=[pltpu.VMEM((nh, LANES), F32)],
        compiler_params=_params("arbitrary"),
        name="logf_cumsum",
    )(lf_t)


def _sample_att_body(pt_ref, q_ref, kn_ref, vn_ref, *rest, n_pages, page, n_kv, g, hd, t_new, fox):
    if fox:
        lfn_ref, kpool, vpool, lfpool, o_ref, kbuf, vbuf, lfbuf, sem = rest
    else:
        kpool, vpool, o_ref, kbuf, vbuf, sem = rest
    b = pl.program_id(0)
    nb = pl.num_programs(0)
    slot = b % 2
    past = n_pages * page
    rows = n_kv * g * t_new

    def copies(bb, sl):
        out = []
        for p in range(n_pages):
            pid = pt_ref[bb, p]
            out.append(pltpu.make_async_copy(kpool.at[pid], kbuf.at[sl, p], sem.at[0, sl]))
            out.append(pltpu.make_async_copy(vpool.at[pid], vbuf.at[sl, p], sem.at[1, sl]))
            if fox:
                out.append(pltpu.make_async_copy(lfpool.at[pid], lfbuf.at[sl, p], sem.at[2, sl]))
        return out

    @pl.when(b == 0)
    def _():
        for cp in copies(0, 0):
            cp.start()

    @pl.when(b + 1 < nb)
    def _():
        for cp in copies(b + 1, 1 - slot):
            cp.start()

    for cp in copies(b, slot):
        cp.wait()

    q = q_ref[0]
    kn = jnp.concatenate([kn_ref[0], jnp.zeros((LANES - t_new, n_kv * hd), F32)], axis=0)
    vn = jnp.concatenate([vn_ref[0], jnp.zeros((LANES - t_new, n_kv * hd), F32)], axis=0)

    z_parts, zn_parts = [], []
    for h in range(n_kv):
        qh = jnp.concatenate([q[:, (h * g + i) * hd:(h * g + i + 1) * hd] for i in range(g)], axis=0)
        kh = kbuf[slot, :, pl.ds(h, page, stride=n_kv), :].reshape(past, hd).astype(BF16)
        z_parts.append(lax.dot_general(qh, kh, (((1,), (1,)), ((), ())), preferred_element_type=F32))
        knh = kn[:, h * hd:(h + 1) * hd].astype(BF16)
        zn_parts.append(lax.dot_general(qh, knh, (((1,), (1,)), ((), ())), preferred_element_type=F32))
    z = jnp.concatenate(z_parts, axis=0)
    zn = jnp.concatenate(zn_parts, axis=0)

    r_io = lax.broadcasted_iota(jnp.int32, (rows, LANES), 0)
    c_io = lax.broadcasted_iota(jnp.int32, (rows, LANES), 1)
    t_row = r_io % t_new

    if not fox:
        valid_n = c_io < t_row
        cn = _cum_dot(jnp.where(valid_n, _softplus2(zn), 0.0), _tri_incl(LANES))
        wn = jnp.where(valid_n, jnp.exp2(zn - cn), 0.0)
        run = cn[:, 0:1]
        tk = CUMSUM_TILE
        tri = _tri_incl(tk)
        w_parts = [None] * (past // tk)
        for j in reversed(range(past // tk)):
            zb = z[:, j * tk:(j + 1) * tk]
            c = _cum_dot(_softplus2(zb), tri)
            w_parts[j] = jnp.exp2(zb - c - run)
            run = run + c[:, 0:1]
        w = jnp.concatenate(w_parts, axis=1)
        denom = None
    else:
        lfn = lfn_ref[0]
        nh = n_kv * g
        tri = _tri_strict(page)
        bias_n = jnp.dot(lfn, tri, preferred_element_type=F32, precision=lax.Precision.HIGHEST)
        carry = jnp.sum(lfn, axis=-1, keepdims=True)
        bias_parts = [None] * n_pages
        for p in reversed(range(n_pages)):
            lfp = lfbuf[slot, p]
            bias_parts[p] = jnp.dot(lfp, tri, preferred_element_type=F32,
                                    precision=lax.Precision.HIGHEST) + carry
            carry = carry + jnp.sum(lfp, axis=-1, keepdims=True)
        bias_h = jnp.concatenate(bias_parts, axis=1)
        bias = jnp.concatenate([jnp.broadcast_to(bias_h[i:i + 1], (t_new, past)) for i in range(nh)], axis=0)
        bias_new = jnp.concatenate([jnp.broadcast_to(bias_n[i:i + 1], (t_new, LANES)) for i in range(nh)], axis=0)
        s = z + bias * LOG2E
        sn = jnp.where(c_io <= t_row, zn + bias_new * LOG2E, NEG_BIG)
        m = jnp.maximum(jnp.max(s, axis=-1, keepdims=True), jnp.max(sn, axis=-1, keepdims=True))
        w = jnp.exp2(s - m)
        wn = jnp.exp2(sn - m)
        denom = jnp.sum(w, axis=-1, keepdims=True) + jnp.sum(wn, axis=-1, keepdims=True)

    wb = w.astype(BF16)
    wnb = wn.astype(BF16)
    o_cols = []
    rh = g * t_new
    for h in range(n_kv):
        vh = vbuf[slot, :, pl.ds(h, page, stride=n_kv), :].reshape(past, hd).astype(BF16)
        vnh = vn[:, h * hd:(h + 1) * hd].astype(BF16)
        oh = (jnp.dot(wb[h * rh:(h + 1) * rh], vh, preferred_element_type=F32)
              + jnp.dot(wnb[h * rh:(h + 1) * rh], vnh, preferred_element_type=F32))
        if fox:
            oh = oh / denom[h * rh:(h + 1) * rh]
        for i in range(g):
            o_cols.append(oh[i * t_new:(i + 1) * t_new])
    o_ref[0] = jnp.concatenate(o_cols, axis=1).astype(BF16)


def _sample_attention(page_table, q_b, k_new, v_new, k_pool, v_pool, lf_new_t, lf_pool_t, *, n_kv, g, hd):
    nb, t_new, dq = q_b.shape
    n_pages = page_table.shape[1]
    page = k_pool.shape[1] // n_kv
    nkv = n_kv * hd
    fox = lf_pool_t is not None
    in_specs = [pl.BlockSpec((1, t_new, dq), lambda b, pt: (b, 0, 0)),
                pl.BlockSpec((1, t_new, nkv), lambda b, pt: (b, 0, 0)),
                pl.BlockSpec((1, t_new, nkv), lambda b, pt: (b, 0, 0))]
    args = [q_b, k_new, v_new]
    scratch = [pltpu.VMEM((2, n_pages, page * n_kv, hd), F32),
               pltpu.VMEM((2, n_pages, page * n_kv, hd), F32)]
    if fox:
        nh = n_kv * g
        in_specs.append(pl.BlockSpec((1, nh, LANES), lambda b, pt: (b, 0, 0)))
        args.append(lf_new_t)
    in_specs += [pl.BlockSpec(memory_space=pl.ANY), pl.BlockSpec(memory_space=pl.ANY)]
    args += [k_pool, v_pool]
    if fox:
        in_specs.append(pl.BlockSpec(memory_space=pl.ANY))
        args.append(lf_pool_t)
        scratch.append(pltpu.VMEM((2, n_pages, nh, page), F32))
    scratch.append(pltpu.SemaphoreType.DMA((3, 2)))
    return pl.pallas_call(
        functools.partial(_sample_att_body, n_pages=n_pages, page=page, n_kv=n_kv, g=g, hd=hd,
                          t_new=t_new, fox=fox),
        grid_spec=pltpu.PrefetchScalarGridSpec(
            num_scalar_prefetch=1, grid=(nb,), in_specs=in_specs,
            out_specs=pl.BlockSpec((1, t_new, dq), lambda b, pt: (b, 0, 0)),
            scratch_shapes=scratch),
        out_shape=jax.ShapeDtypeStruct((nb, t_new, dq), BF16),
        compiler_params=_params("arbitrary"),
        name="fox_sample_attention" if fox else "sb_sample_attention",
    )(page_table, *args)


def _layernorm(u, gamma, beta):
    mu = jnp.mean(u, axis=-1, keepdims=True)
    uc = u - mu
    var = jnp.mean(uc * uc, axis=-1, keepdims=True)
    return uc * lax.rsqrt(var + LN_EPS) * gamma + beta


def _first_index_of_max(vals, mask, lane):
    top = jnp.max(jnp.where(mask, vals, NEG_BIG), axis=-1, keepdims=True)
    idx = jnp.min(jnp.where(mask & (vals == top), lane, 2 * LANES), axis=-1, keepdims=True)
    return top, idx


def _post_attn_body(op_ref, os_ref, x_ref, g1_ref, sh2_ref, sc2_ref, w_ref, lng_ref, lnb_ref, rw_ref, rb_ref,
                    x1_ref, h2_ref, eid_ref, gate_ref, *, alpha, n_exp, n_prompt_tiles):
    o = jnp.where(pl.program_id(0) < n_prompt_tiles, op_ref[...], os_ref[...])
    y = jnp.dot(o, w_ref[...], preferred_element_type=F32)
    x1 = _layernorm(alpha * x_ref[...] + g1_ref[0] * y, lng_ref[...], lnb_ref[...])
    x1_ref[...] = x1
    h2 = x1 * (1.0 + sc2_ref[0]) + sh2_ref[0]
    h2_ref[...] = h2

    logits = _dot_split(h2, rw_ref[...])
    lane = lax.broadcasted_iota(jnp.int32, logits.shape, 1)
    real = lane < n_exp
    logits = jnp.where(real, logits, NEG_BIG)
    e = jnp.exp(logits - jnp.max(logits, axis=-1, keepdims=True))
    probs = e / jnp.sum(e, axis=-1, keepdims=True)
    sel = probs + rb_ref[...]
    epg = n_exp // N_GROUPS
    best = first = second = None
    for grp in range(N_GROUPS):
        in_g = (lane >= grp * epg) & (lane < (grp + 1) * epg)
        top1, idx1 = _first_index_of_max(sel, in_g, lane)
        top2, idx2 = _first_index_of_max(sel, in_g & (lane != idx1), lane)
        score = top1 + top2
        if grp == 0:
            best, first, second = score, idx1, idx2
        else:
            better = score > best
            best = jnp.where(better, score, best)
            first = jnp.where(better, idx1, first)
            second = jnp.where(better, idx2, second)
    gate1 = jnp.sum(jnp.where(lane == first, probs, 0.0), axis=-1, keepdims=True)
    gate2 = jnp.sum(jnp.where(lane == second, probs, 0.0), axis=-1, keepdims=True)
    tot = gate1 + gate2
    eid_ref[...] = jnp.where(lane == 0, first, jnp.where(lane == 1, second, 0))
    gate_ref[...] = jnp.where(lane == 0, gate1 / tot, jnp.where(lane == 1, gate2 / tot, 0.0))


def _post_attention(o_p, o_s, x, mod, layer, w_out, ln_g, ln_b, rw_pad, rb_pad, *, n_prompt_tiles, alpha, n_exp):
    n, d = x.shape
    tm = TOKEN_TILE
    kind = lambda i: i // n_prompt_tiles
    row = pl.BlockSpec((tm, d), lambda i: (i, 0))
    row_p = pl.BlockSpec((tm, d), lambda i: (jnp.minimum(i, n_prompt_tiles - 1), 0))
    row_s = pl.BlockSpec((tm, d), lambda i: (jnp.maximum(i - n_prompt_tiles, 0), 0))
    modspec = lambda c: pl.BlockSpec((1, tm, d), lambda i: (layer, kind(i), c))
    const = lambda a: pl.BlockSpec(a.shape, lambda i: (0,) * a.ndim)
    lanes = pl.BlockSpec((tm, LANES), lambda i: (i, 0))
    return pl.pallas_call(
        functools.partial(_post_attn_body, alpha=alpha, n_exp=n_exp, n_prompt_tiles=n_prompt_tiles),
        grid=(n // tm,),
        in_specs=[row_p, row_s, row, modspec(2), modspec(3), modspec(4), const(w_out), const(ln_g), const(ln_b),
                  const(rw_pad), const(rb_pad)],
        out_specs=[row, row, lanes, lanes],
        out_shape=[jax.ShapeDtypeStruct((n, d), F32), jax.ShapeDtypeStruct((n, d), F32),
                   jax.ShapeDtypeStruct((n, LANES), jnp.int32), jax.ShapeDtypeStruct((n, LANES), F32)],
        compiler_params=_params("arbitrary"),
        name="post_attention",
    )(o_p, o_s, x, mod, mod, mod, w_out, ln_g, ln_b, rw_pad, rb_pad)


def _moe_body(texp_ref, trows_ref, src_ref, dst_ref, srcn_ref, gate_ref, wg_ref, wu_ref, wd_ref,
              h_hbm, out_hbm, xbuf, ybuf, sem, *, tm):
    t = pl.program_id(0)
    nt = pl.num_programs(0)
    slot = t % 2
    del texp_ref
    rows_of = lambda i: trows_ref[jnp.clip(i, 0, nt - 1)]

    def start_gather(idx_ref, sl):
        def chunk(c, _):
            for u in range(ROW_DMA_UNROLL):
                r = c * ROW_DMA_UNROLL + u
                pltpu.make_async_copy(h_hbm.at[idx_ref[0, 0, r]], xbuf.at[sl, r], sem.at[0, sl]).start()
            return 0
        lax.fori_loop(0, tm // ROW_DMA_UNROLL, chunk, 0)

    def wait_gather(sl):
        pltpu.make_async_copy(h_hbm.at[pl.ds(0, tm)], xbuf.at[sl], sem.at[0, sl]).wait()

    def start_scatter(sl, n_rows):
        def one(r):
            pltpu.make_async_copy(ybuf.at[sl, r], out_hbm.at[dst_ref[0, 0, r]], sem.at[1, sl]).start()

        def chunk(c, _):
            for u in range(ROW_DMA_UNROLL):
                one(c * ROW_DMA_UNROLL + u)
            return 0
        n_full = n_rows // ROW_DMA_UNROLL
        lax.fori_loop(0, n_full, chunk, 0)
        lax.fori_loop(n_full * ROW_DMA_UNROLL, n_rows, lambda r, _: (one(r), 0)[1], 0)

    def wait_scatter(sl, n_rows):
        p = tm
        while p >= 1:
            @pl.when((n_rows & p) != 0)
            def _(p=p):
                pltpu.make_async_copy(ybuf.at[sl, pl.ds(0, p)], out_hbm.at[pl.ds(0, p)], sem.at[1, sl]).wait()
            p //= 2

    @pl.when((t == 0) & (rows_of(0) > 0))
    def _():
        start_gather(src_ref, 0)

    @pl.when((t + 1 < nt) & (rows_of(t + 1) > 0))
    def _():
        start_gather(srcn_ref, 1 - slot)

    @pl.when(t >= 2)
    def _():
        wait_scatter(slot, rows_of(t - 2))

    @pl.when(rows_of(t) > 0)
    def _():
        wait_gather(slot)
        xb = xbuf[slot].astype(BF16)
        gp = jnp.dot(xb, wg_ref[0, 0].astype(BF16), preferred_element_type=F32)
        up = jnp.dot(xb, wu_ref[0, 0].astype(BF16), preferred_element_type=F32)
        act = gp / (1.0 + jnp.exp(-gp)) * up * gate_ref[...]
        ybuf[slot] = jnp.dot(act.astype(BF16), wd_ref[0, 0].astype(BF16), preferred_element_type=F32)
        start_scatter(slot, rows_of(t))

    @pl.when(t == nt - 1)
    def _():
        @pl.when(nt >= 2)
        def _():
            wait_scatter(1 - slot, rows_of(t - 1))
        wait_scatter(slot, rows_of(t))


def _moe(h2, tile_expert, tile_rows, row_src, row_dst, row_gate, w_gate, w_up, w_down, layer):
    n, d = h2.shape
    tm = MOE_ROW_TILE
    n_tiles = tile_expert.shape[0]
    f = w_gate.shape[-1]
    nxt = lambda t: jnp.minimum(t + 1, n_tiles - 1)
    smem_rows = lambda imap: pl.BlockSpec((1, 1, tm), imap, memory_space=pltpu.SMEM)
    return pl.pallas_call(
        functools.partial(_moe_body, tm=tm),
        grid_spec=pltpu.PrefetchScalarGridSpec(
            num_scalar_prefetch=2, grid=(n_tiles,),
            in_specs=[smem_rows(lambda t, te, tr: (t, 0, 0)),
                      smem_rows(lambda t, te, tr: (t, 0, 0)),
                      smem_rows(lambda t, te, tr: (nxt(t), 0, 0)),
                      pl.BlockSpec((tm, 1), lambda t, te, tr: (t, 0)),
                      pl.BlockSpec((1, 1, d, f), lambda t, te, tr: (layer, te[t], 0, 0)),
                      pl.BlockSpec((1, 1, d, f), lambda t, te, tr: (layer, te[t], 0, 0)),
                      pl.BlockSpec((1, 1, f, d), lambda t, te, tr: (layer, te[t], 0, 0)),
                      pl.BlockSpec(memory_space=pl.ANY)],
            out_specs=pl.BlockSpec(memory_space=pl.ANY),
            scratch_shapes=[pltpu.VMEM((2, tm, d), F32), pltpu.VMEM((2, tm, d), F32),
                            pltpu.SemaphoreType.DMA((2, 2))]),
        out_shape=jax.ShapeDtypeStruct((TOP_K * n, d), F32),
        compiler_params=_params("arbitrary"),
        name="sparse_experts",
    )(tile_expert, tile_rows, row_src, row_dst, row_src, row_gate, w_gate, w_up, w_down, h2)


def _routing_tables(eid, gate, n_exp, tm):
    n = eid.shape[0]
    n_assign = n * TOP_K
    n_tiles = (n_assign + n_exp * (tm - 1)) // tm + 1
    e_flat = eid.reshape(n_assign)
    order = jnp.argsort(e_flat, stable=True).astype(jnp.int32)
    counts = jnp.sum((e_flat[:, None] == jnp.arange(n_exp, dtype=jnp.int32)[None, :]).astype(jnp.int32), axis=0)
    starts = jnp.cumsum(counts) - counts
    tiles_per = (counts + tm - 1) // tm
    tile_starts = jnp.cumsum(tiles_per) - tiles_per
    n_used = jnp.sum(tiles_per).astype(jnp.int32)
    tile_ids = jnp.arange(n_tiles, dtype=jnp.int32)
    tile_ends = tile_starts + tiles_per
    tile_expert = jnp.sum((tile_ids[:, None] >= tile_ends[None, :]).astype(jnp.int32), axis=1)
    tile_expert = jnp.minimum(tile_expert, n_exp - 1)
    row_in_tile = jnp.arange(tm, dtype=jnp.int32)[None, :]
    within = (tile_ids - tile_starts[tile_expert])[:, None] * tm + row_in_tile
    valid = (within < counts[tile_expert][:, None]) & (tile_ids[:, None] < n_used)
    sorted_pos = jnp.clip(starts[tile_expert][:, None] + within, 0, n_assign - 1)
    assign = order[sorted_pos]
    tok = assign // TOP_K
    slot_k = assign % TOP_K
    row_src = jnp.where(valid, tok, 0).astype(jnp.int32)
    row_dst = jnp.where(valid, slot_k * n + tok, 0).astype(jnp.int32)
    row_gate = jnp.where(valid, gate.reshape(n_assign)[assign], 0.0).astype(F32)
    tile_rows = jnp.sum(valid.astype(jnp.int32), axis=1)
    return (tile_expert, tile_rows, row_src.reshape(n_tiles, 1, tm),
            row_dst.reshape(n_tiles, 1, tm), row_gate.reshape(n_tiles * tm, 1))


def _combine_body(x1_ref, ya_ref, yb_ref, g2_ref, lng_ref, lnb_ref, o_ref, *, alpha):
    y = ya_ref[...] + yb_ref[...]
    o_ref[...] = _layernorm(alpha * x1_ref[...] + g2_ref[0] * y, lng_ref[...], lnb_ref[...])


def _combine(x1, y2, mod, layer, ln_g, ln_b, *, n_prompt_tiles, alpha):
    n, d = x1.shape
    tm = TOKEN_TILE
    kind = lambda i: i // n_prompt_tiles
    row = pl.BlockSpec((tm, d), lambda i: (i, 0))
    const = lambda a: pl.BlockSpec(a.shape, lambda i: (0,) * a.ndim)
    return pl.pallas_call(
        functools.partial(_combine_body, alpha=alpha),
        grid=(n // tm,),
        in_specs=[row, row, pl.BlockSpec((tm, d), lambda i: (n // tm + i, 0)),
                  pl.BlockSpec((1, tm, d), lambda i: (layer, kind(i), 5)), const(ln_g), const(ln_b)],
        out_specs=row,
        out_shape=jax.ShapeDtypeStruct((n, d), F32),
        compiler_params=_params("arbitrary"),
        name="combine_layernorm",
    )(x1, y2, y2, mod, ln_g, ln_b)


def kernel(x_prompt, x_sample, cache_k_sb, cache_v_sb, cache_k_fox, cache_v_fox, cache_logf_fox, page_table,
           c_prompt, c_sample, ada_w, ada_b, ln_g, ln_b, w_in_sb, w_out_sb, w_in_fox, b_f_fox, w_out_fox,
           router_w, router_b, w_gate, w_up, w_down):
    batch, seq, d = x_prompt.shape
    nb, t_new, _ = x_sample.shape
    depth = ada_w.shape[0]
    n_pool, page, n_kv, hd = cache_k_sb.shape[1:]
    n_heads = d // hd
    g = n_heads // n_kv
    nq, nkv = n_heads * hd, n_kv * hd
    n_exp = router_w.shape[1]
    scale = float(hd) ** -0.5 * LOG2E
    alpha = (2.0 * depth) ** 0.25
    tm = TOKEN_TILE
    n_p, n_s = batch * seq, nb * t_new
    n = n_p + n_s
    assert batch == 1 and n_p % tm == 0 and n_s % tm == 0 and tm % nb == 0
    assert seq % ATT_KV_TILE == 0 and seq % FOX_KV_TILE == 0 and seq % ATT_Q_TILE == 0
    assert ATT_KV_TILE % ATT_Q_TILE == 0 and FOX_KV_TILE % ATT_Q_TILE == 0 and ATT_KV_TILE % CUMSUM_TILE == 0
    assert (page * page_table.shape[1]) % CUMSUM_TILE == 0
    assert n_exp % N_GROUPS == 0 and n_exp <= LANES and n_heads <= LANES and t_new <= LANES
    n_prompt_tiles = n_p // tm

    xs_t = jnp.transpose(x_sample, (1, 0, 2)).reshape(n_s, d)
    x = jnp.concatenate([x_prompt.reshape(n_p, d), xs_t], axis=0)
    c_rows = jnp.concatenate([jnp.broadcast_to(c_prompt, (tm, d))] + [c_sample] * (tm // nb), axis=0)
    mod = _modulation(c_rows, ada_w, ada_b)

    rw_pad = _hi_lo_columns(router_w)
    rb_pad = jnp.pad(router_b.reshape(1, n_exp), ((0, 0), (0, LANES - n_exp)))

    def to_seq_major(a):
        return jnp.transpose(a.reshape(t_new, nb, a.shape[-1]), (1, 0, 2))

    def to_pos_major(a):
        return jnp.transpose(a, (1, 0, 2)).reshape(n_s, a.shape[-1])

    sb_out, fox_out = [], []
    for layer in range(depth):
        j = layer // 2
        fox = layer % 2 == 1
        if fox:
            w_in, w_out = w_in_fox[j], w_out_fox[j]
            w_main = w_in[:, :nq + 2 * nkv].astype(BF16)
            w_f = _hi_lo_columns(w_in[:, nq + 2 * nkv:])
            b_f = jnp.pad(b_f_fox[j].reshape(1, n_heads), ((0, 0), (0, LANES - n_heads)))
            k_pool, v_pool = cache_k_fox[j], cache_v_fox[j]
        else:
            w_in, w_out = w_in_sb[j], w_out_sb[j]
            w_main, w_f, b_f = w_in.astype(BF16), None, None
            k_pool, v_pool = cache_k_sb[j], cache_v_sb[j]
        res = _qkv_proj(x, mod, layer, w_main, w_f, b_f, n_prompt_tiles=n_prompt_tiles, nb=nb,
                        nq=nq, nkv=nkv, scale=scale)
        q_p, k_p, v_p, kb_p, vb_p = res[:5]
        q_s = res[5].reshape(nb, t_new, nq)
        k_s, v_s = res[6].reshape(nb, t_new, nkv), res[7].reshape(nb, t_new, nkv)
        k_pool = k_pool.reshape(n_pool, page * n_kv, hd)
        v_pool = v_pool.reshape(n_pool, page * n_kv, hd)
        kv_out = (k_p.reshape(batch, seq, n_kv, hd), v_p.reshape(batch, seq, n_kv, hd),
                  k_s.reshape(nb, t_new, n_kv, hd), v_s.reshape(nb, t_new, n_kv, hd))
        if fox:
            lf_p = res[8][:, :n_heads]
            lf_s = res[9].reshape(nb, t_new, LANES)[:, :, :n_heads]
            ck_t = _cumsum_lanes(jnp.transpose(lf_p))
            lf_new_t = jnp.pad(jnp.transpose(lf_s, (0, 2, 1)), ((0, 0), (0, 0), (0, LANES - t_new)))
            lf_pool_t = jnp.transpose(cache_logf_fox[j], (0, 2, 1))
            o_p = _prompt_attention(q_p, kb_p, vb_p, ck_t, seq=seq, n_kv=n_kv, g=g, hd=hd)
            o_s = _sample_attention(page_table, q_s, k_s, v_s, k_pool, v_pool, lf_new_t, lf_pool_t,
                                    n_kv=n_kv, g=g, hd=hd)
            fox_out.append(kv_out[:2] + (lf_p.reshape(batch, seq, n_heads),) + kv_out[2:] + (lf_s,))
        else:
            o_p = _prompt_attention(q_p, kb_p, vb_p, None, seq=seq, n_kv=n_kv, g=g, hd=hd)
            o_s = _sample_attention(page_table, q_s, k_s, v_s, k_pool, v_pool, None, None,
                                    n_kv=n_kv, g=g, hd=hd)
            sb_out.append(kv_out)
        x1, h2, eid, gate = _post_attention(
            o_p, to_pos_major(o_s), x, mod, layer, w_out.astype(BF16), ln_g[layer, 0].reshape(1, d), ln_b[layer, 0].reshape(1, d),
            rw_pad, rb_pad, n_prompt_tiles=n_prompt_tiles, alpha=alpha, n_exp=n_exp)
        tables = _routing_tables(eid[:, :TOP_K], gate[:, :TOP_K], n_exp, MOE_ROW_TILE)
        y2 = _moe(h2, *tables, w_gate, w_up, w_down, layer)
        x = _combine(x1, y2, mod, layer, ln_g[layer, 1].reshape(1, d), ln_b[layer, 1].reshape(1, d),
                     n_prompt_tiles=n_prompt_tiles, alpha=alpha)

    y_prompt = x[:n_p].reshape(batch, seq, d)
    y_sample = to_seq_major(x[n_p:])
    stack = lambda items, i: jnp.stack([e[i] for e in items])
    return (y_prompt, y_sample,
            stack(sb_out, 0), stack(sb_out, 1), stack(sb_out, 2), stack(sb_out, 3),
            stack(fox_out, 0), stack(fox_out, 1), stack(fox_out, 2),
            stack(fox_out, 3), stack(fox_out, 4), stack(fox_out, 5))
```

```python
import functools

import jax
import jax.numpy as jnp
from jax import lax
from jax.experimental import pallas as pl
from jax.experimental.pallas import tpu as pltpu

F32 = jnp.float32
BF16 = jnp.bfloat16

N_GROUPS = 4
TOP_K = 2
LN_EPS = 1e-5
NEG_BIG = -1e30

TOKEN_TILE = 256
MOE_ROW_TILE = 256
ROW_DMA_UNROLL = 8
ATT_Q_TILE = 256
CUMSUM_TILE = 256
ATT_KV_TILE = 512
FOX_KV_TILE = 512
LANES = 128
V7X_VMEM_LIMIT = 56 * 1024 * 1024


def _params(*sem):
    return pltpu.CompilerParams(dimension_semantics=sem, vmem_limit_bytes=V7X_VMEM_LIMIT)


LOG2E = 1.4426950408889634


def _log_sigmoid_neg(z):
    return -(jnp.maximum(z, 0.0) + jnp.log(1.0 + jnp.exp(-jnp.abs(z))))


def _softplus2(z2):
    neg_abs = pltpu.bitcast(pltpu.bitcast(z2, jnp.uint32) | jnp.uint32(0x80000000), F32)
    return jnp.maximum(z2, 0.0) + jnp.log2(1.0 + jnp.exp2(neg_abs))


def _cum_dot(a, tri):
    return jnp.dot(a.astype(BF16), tri, preferred_element_type=F32)


def _hi_lo_columns(w):
    w = jnp.pad(w, ((0, 0), (0, LANES - w.shape[1])))
    hi = w.astype(BF16)
    lo = (w - hi.astype(F32)).astype(BF16)
    return jnp.concatenate([hi, lo], axis=1)


def _dot_split(a, w_hl):
    hi = a.astype(BF16)
    lo = (a - hi.astype(F32)).astype(BF16)
    p = jnp.dot(hi, w_hl, preferred_element_type=F32)
    q = jnp.dot(lo, w_hl[:, :LANES], preferred_element_type=F32)
    return p[:, :LANES] + p[:, LANES:] + q


def _tri_incl(n):
    r = lax.broadcasted_iota(jnp.int32, (n, n), 0)
    c = lax.broadcasted_iota(jnp.int32, (n, n), 1)
    return (r >= c).astype(BF16)


def _tri_strict(n):
    r = lax.broadcasted_iota(jnp.int32, (n, n), 0)
    c = lax.broadcasted_iota(jnp.int32, (n, n), 1)
    return (r > c).astype(F32)


def _mod_body(c_ref, w_ref, b_ref, o_ref):
    c = c_ref[...]
    a = c / (1.0 + jnp.exp(-c))
    o_ref[0] = jnp.dot(a.astype(BF16), w_ref[0].astype(BF16),
                       preferred_element_type=F32) + b_ref[0]


def _modulation(c_rows, ada_w, ada_b):
    depth, d, d6 = ada_w.shape
    rows = c_rows.shape[0]
    tn = min(1024, d6)
    return pl.pallas_call(
        _mod_body,
        grid=(depth, d6 // tn),
        in_specs=[pl.BlockSpec((rows, d), lambda l, j: (0, 0)),
                  pl.BlockSpec((1, d, tn), lambda l, j: (l, 0, j)),
                  pl.BlockSpec((1, 1, tn), lambda l, j: (l, 0, j))],
        out_specs=pl.BlockSpec((1, rows, tn), lambda l, j: (l, 0, j)),
        out_shape=jax.ShapeDtypeStruct((depth, rows, d6), F32),
        compiler_params=_params("arbitrary", "arbitrary"),
        name="modulation",
    )(c_rows, ada_w, ada_b.reshape(depth, 1, d6))


def _prompt_rows(tm, width, n_prompt_tiles):
    return pl.BlockSpec((tm, width), lambda i: (jnp.minimum(i, n_prompt_tiles - 1), 0))


def _sample_rows(tm, width, n_prompt_tiles):
    return pl.BlockSpec((tm, width), lambda i: (jnp.maximum(i - n_prompt_tiles, 0), 0))


def _qkv_body(*refs, nq, nkv, scale, fox, n_prompt_tiles, nb):
    if fox:
        (xp_ref, xs_ref, sh_ref, sc_ref, w_ref, wf_ref, bf_ref, qp_ref, kp_ref, vp_ref, kbp_ref, vbp_ref,
         qs_ref, ks_ref, vs_ref, lfp_ref, lfs_ref) = refs
    else:
        (xp_ref, xs_ref, sh_ref, sc_ref, w_ref, qp_ref, kp_ref, vp_ref, kbp_ref, vbp_ref,
         qs_ref, ks_ref, vs_ref) = refs
    i = pl.program_id(0)
    x = jnp.where(i < n_prompt_tiles, xp_ref[...], xs_ref[...])
    h = x * (1.0 + sc_ref[0]) + sh_ref[0]
    p = jnp.dot(h.astype(BF16), w_ref[...], preferred_element_type=F32)
    q = (p[:, :nq] * scale).astype(BF16)
    k = p[:, nq:nq + nkv]
    v = p[:, nq + nkv:nq + 2 * nkv]
    if fox:
        lf = _log_sigmoid_neg(-(_dot_split(h, wf_ref[...]) + bf_ref[...]))

    @pl.when(i < n_prompt_tiles)
    def _():
        qp_ref[...] = q
        n_kv = nkv // kp_ref.shape[1]
        for hh in range(n_kv):
            cols = slice(hh * kp_ref.shape[1], (hh + 1) * kp_ref.shape[1])
            kp_ref[pl.ds(hh, k.shape[0], stride=n_kv), :] = k[:, cols]
            vp_ref[pl.ds(hh, v.shape[0], stride=n_kv), :] = v[:, cols]
        kbp_ref[...] = k.astype(BF16)
        vbp_ref[...] = v.astype(BF16)
        if fox:
            lfp_ref[...] = lf

    @pl.when(i >= n_prompt_tiles)
    def _():
        for t in range(q.shape[0] // nb):
            rows = slice(t * nb, (t + 1) * nb)
            qs_ref[:, t * nq:(t + 1) * nq] = q[rows]
            ks_ref[:, t * nkv:(t + 1) * nkv] = k[rows]
            vs_ref[:, t * nkv:(t + 1) * nkv] = v[rows]
            if fox:
                lfs_ref[:, t * LANES:(t + 1) * LANES] = lf[rows]


def _qkv_proj(x_p, x_s, mod, layer, w_main, w_f, b_f, *, nb, nq, nkv, hd, scale):
    n_p, d = x_p.shape
    n_s = x_s.shape[0]
    tm = TOKEN_TILE
    fox = w_f is not None
    n_prompt_tiles = n_p // tm
    n = n_p + n_s
    t_new = n_s // nb
    per = tm // nb
    kind = lambda i: i // n_prompt_tiles
    in_specs = [_prompt_rows(tm, d, n_prompt_tiles), _sample_rows(tm, d, n_prompt_tiles),
                pl.BlockSpec((1, tm, d), lambda i: (layer, kind(i), 0)),
                pl.BlockSpec((1, tm, d), lambda i: (layer, kind(i), 1)),
                pl.BlockSpec(w_main.shape, lambda i: (0, 0))]
    args = [x_p, x_s, mod, mod, w_main]
    if fox:
        in_specs += [pl.BlockSpec(w_f.shape, lambda i: (0, 0)),
                     pl.BlockSpec(b_f.shape, lambda i: (0, 0))]
        args += [w_f, b_f]
    n_kv = nkv // hd
    p_spec = lambda w: _prompt_rows(tm, w, n_prompt_tiles)
    kv_spec = _prompt_rows(tm * n_kv, hd, n_prompt_tiles)
    s_spec = lambda w: pl.BlockSpec((nb, per * w), lambda i: (0, jnp.maximum(i - n_prompt_tiles, 0)))
    p_shape = lambda w, dt: jax.ShapeDtypeStruct((n_p, w), dt)
    kv_shape = jax.ShapeDtypeStruct((n_p * n_kv, hd), F32)
    s_shape = lambda w, dt: jax.ShapeDtypeStruct((nb, t_new * w), dt)
    out_specs = [p_spec(nq), kv_spec, kv_spec, p_spec(nkv), p_spec(nkv),
                 s_spec(nq), s_spec(nkv), s_spec(nkv)]
    out_shape = [p_shape(nq, BF16), kv_shape, kv_shape, p_shape(nkv, BF16), p_shape(nkv, BF16),
                 s_shape(nq, BF16), s_shape(nkv, F32), s_shape(nkv, F32)]
    if fox:
        out_specs += [p_spec(LANES), s_spec(LANES)]
        out_shape += [p_shape(LANES, F32), s_shape(LANES, F32)]
    return pl.pallas_call(
        functools.partial(_qkv_body, nq=nq, nkv=nkv, scale=scale, fox=fox,
                          n_prompt_tiles=n_prompt_tiles, nb=nb),
        grid=(n // tm,),
        in_specs=in_specs, out_specs=out_specs, out_shape=out_shape,
        compiler_params=_params("arbitrary"),
        name="qkv_fox" if fox else "qkv_sb",
    )(*args)


def _stack_heads(q, g, hd):
    return jnp.concatenate([q[:, i * hd:(i + 1) * hd] for i in range(g)], axis=0)


def _unstack_heads(o, g, tq):
    return jnp.concatenate([o[i * tq:(i + 1) * tq] for i in range(g)], axis=1)


def _causal_sweep(qi, tq, tk, carry, block):
    assert tk == 2 * tq
    carry = block(qi * tq, tq, carry, True)
    odd = qi % 2
    carry = lax.cond(odd == 1, lambda c: block((qi - 1) * tq, tq, c, False), lambda c: c, carry)
    n_wide = qi // 2

    def pair(i, c):
        j = n_wide - 1 - 2 * i
        return block((j - 1) * tk, tk, block(j * tk, tk, c, False), False)
    carry = lax.fori_loop(0, n_wide // 2, pair, carry)
    return lax.cond(n_wide % 2 == 1, lambda c: block(0, tk, c, False), lambda c: c, carry)


def _sb_prompt_body(q_ref, k_ref, v_ref, o_ref, *, tq, tk, g, hd):
    qi = pl.program_id(1)
    rows = g * tq
    qs = _stack_heads(q_ref[...], g, hd)
    tc = CUMSUM_TILE
    tri = _tri_incl(tc)

    def block(start, width, carry, masked):
        run, acc = carry
        start = start if isinstance(start, int) else pl.multiple_of(start, tq)
        kb = k_ref[pl.ds(start, width), :]
        vb = v_ref[pl.ds(start, width), :]
        z = lax.dot_general(qs, kb, (((1,), (1,)), ((), ())), preferred_element_type=F32)
        sp = _softplus2(z)
        if masked:
            r_io = lax.broadcasted_iota(jnp.int32, (rows, width), 0)
            c_io = lax.broadcasted_iota(jnp.int32, (rows, width), 1)
            valid = c_io < (r_io & (tq - 1))
            sp = jnp.where(valid, sp, 0.0)
        w_parts = [None] * (width // tc)
        for i in reversed(range(width // tc)):
            c = _cum_dot(sp[:, i * tc:(i + 1) * tc], tri)
            w_parts[i] = jnp.exp2(z[:, i * tc:(i + 1) * tc] - c - run)
            run = run + c[:, 0:1]
        w = jnp.concatenate(w_parts, axis=1)
        if masked:
            w = jnp.where(valid, w, 0.0)
        acc = acc + jnp.dot(w.astype(BF16), vb, preferred_element_type=F32)
        return run, acc

    carry = (jnp.zeros((rows, 1), F32), jnp.zeros((rows, hd), F32))
    _, acc = _causal_sweep(qi, tq, tk, carry, block)
    o_ref[...] = _unstack_heads(acc, g, tq).astype(BF16)


def _fox_prompt_body(q_ref, k_ref, v_ref, ck_ref, o_ref, *, tq, tk, g, hd):
    h = pl.program_id(0)
    qi = pl.program_id(1)
    rows = g * tq
    qs = _stack_heads(q_ref[...], g, hd)

    def block(start, width, carry, masked):
        m, l, acc = carry
        start = start if isinstance(start, int) else pl.multiple_of(start, tq)
        kb = k_ref[pl.ds(start, width), :]
        vb = v_ref[pl.ds(start, width), :]
        z = lax.dot_general(qs, kb, (((1,), (1,)), ((), ())), preferred_element_type=F32)
        s = jnp.concatenate(
            [z[i * tq:(i + 1) * tq] - ck_ref[pl.ds(h * g + i, 1), pl.ds(start, width)] for i in range(g)], axis=0)
        if masked:
            r_io = lax.broadcasted_iota(jnp.int32, (rows, width), 0)
            c_io = lax.broadcasted_iota(jnp.int32, (rows, width), 1)
            s = jnp.where(c_io <= (r_io & (tq - 1)), s, NEG_BIG)
        m_new = jnp.maximum(m, jnp.max(s, axis=-1, keepdims=True))
        alpha = jnp.exp2(m - m_new)
        p = jnp.exp2(s - m_new)
        l = alpha * l + jnp.sum(p, axis=-1, keepdims=True)
        acc = alpha * acc + jnp.dot(p.astype(BF16), vb, preferred_element_type=F32)
        return m_new, l, acc

    carry = (jnp.full((rows, 1), NEG_BIG, F32), jnp.zeros((rows, 1), F32), jnp.zeros((rows, hd), F32))
    _, l, acc = _causal_sweep(qi, tq, tk, carry, block)
    o_ref[...] = _unstack_heads(acc / l, g, tq).astype(BF16)


def _prompt_attention(q, kb, vb, ck_t, *, seq, n_kv, g, hd):
    fox = ck_t is not None
    tq, tk = ATT_Q_TILE, (FOX_KV_TILE if fox else ATT_KV_TILE)
    in_specs = [pl.BlockSpec((tq, g * hd), lambda h, i: (i, h)),
                pl.BlockSpec((seq, hd), lambda h, i: (0, h)),
                pl.BlockSpec((seq, hd), lambda h, i: (0, h))]
    args = [q, kb, vb]
    if fox:
        in_specs.append(pl.BlockSpec(ck_t.shape, lambda h, i: (0, 0)))
        args.append(ck_t)
    body = _fox_prompt_body if fox else _sb_prompt_body
    return pl.pallas_call(
        functools.partial(body, tq=tq, tk=tk, g=g, hd=hd),
        grid=(n_kv, seq // tq),
        in_specs=in_specs,
        out_specs=pl.BlockSpec((tq, g * hd), lambda h, i: (i, h)),
        out_shape=jax.ShapeDtypeStruct((seq, n_kv * g * hd), BF16),
        compiler_params=_params("arbitrary", "arbitrary"),
        name="fox_prompt_attention" if fox else "sb_prompt_attention",
    )(*args)


def _cumsum_body(lf_ref, o_ref, carry_ref, *, tk):
    @pl.when(pl.program_id(0) == 0)
    def _():
        carry_ref[...] = jnp.zeros_like(carry_ref)
    r = lax.broadcasted_iota(jnp.int32, (tk, tk), 0)
    c = lax.broadcasted_iota(jnp.int32, (tk, tk), 1)
    upper = (r <= c).astype(F32)
    cum = jnp.dot(lf_ref[...], upper, preferred_element_type=F32,
                  precision=lax.Precision.HIGHEST) + carry_ref[:, 0:1]
    o_ref[...] = cum * LOG2E
    carry_ref[...] = jnp.broadcast_to(cum[:, tk - 1:tk], carry_ref.shape)


def _cumsum_lanes(lf_t):
    nh, seq = lf_t.shape
    tk = CUMSUM_TILE
    return pl.pallas_call(
        functools.partial(_cumsum_body, tk=tk),
        grid=(seq // tk,),
        in_specs=[pl.BlockSpec((nh, tk), lambda j: (0, j))],
        out_specs=pl.BlockSpec((nh, tk), lambda j: (0, j)),
        out_shape=jax.ShapeDtypeStruct((nh, seq), F32),
        scratch_shapes=[pltpu.VMEM((nh, LANES), F32)],
        compiler_params=_params("arbitrary"),
        name="logf_cumsum",
    )(lf_t)


def _sample_att_body(pt_ref, q_ref, kn_ref, vn_ref, *rest, n_pages, page, n_kv, g, hd, t_new, fox):
    if fox:
        lfn_ref, kpool, vpool, lfpool, o_ref, kbuf, vbuf, lfbuf, sem = rest
    else:
        kpool, vpool, o_ref, kbuf, vbuf, sem = rest
    b = pl.program_id(0)
    nb = pl.num_programs(0)
    slot = b % 2
    past = n_pages * page
    rows = n_kv * g * t_new

    def copies(bb, sl):
        out = []
        for p in range(n_pages):
            pid = pt_ref[bb, p]
            out.append(pltpu.make_async_copy(kpool.at[pid], kbuf.at[sl, p], sem.at[0, sl]))
            out.append(pltpu.make_async_copy(vpool.at[pid], vbuf.at[sl, p], sem.at[1, sl]))
            if fox:
                out.append(pltpu.make_async_copy(lfpool.at[pid], lfbuf.at[sl, p], sem.at[2, sl]))
        return out

    @pl.when(b == 0)
    def _():
        for cp in copies(0, 0):
            cp.start()

    @pl.when(b + 1 < nb)
    def _():
        for cp in copies(b + 1, 1 - slot):
            cp.start()

    for cp in copies(b, slot):
        cp.wait()

    q = q_ref[0]
    kn = jnp.concatenate([kn_ref[0], jnp.zeros((LANES - t_new, n_kv * hd), F32)], axis=0)
    vn = jnp.concatenate([vn_ref[0], jnp.zeros((LANES - t_new, n_kv * hd), F32)], axis=0)

    z_parts, zn_parts = [], []
    for h in range(n_kv):
        qh = jnp.concatenate([q[:, (h * g + i) * hd:(h * g + i + 1) * hd] for i in range(g)], axis=0)
        kh = kbuf[slot, :, pl.ds(h, page, stride=n_kv), :].reshape(past, hd).astype(BF16)
        z_parts.append(lax.dot_general(qh, kh, (((1,), (1,)), ((), ())), preferred_element_type=F32))
        knh = kn[:, h * hd:(h + 1) * hd].astype(BF16)
        zn_parts.append(lax.dot_general(qh, knh, (((1,), (1,)), ((), ())), preferred_element_type=F32))
    z = jnp.concatenate(z_parts, axis=0)
    zn = jnp.concatenate(zn_parts, axis=0)

    r_io = lax.broadcasted_iota(jnp.int32, (rows, LANES), 0)
    c_io = lax.broadcasted_iota(jnp.int32, (rows, LANES), 1)
    t_row = r_io % t_new

    if not fox:
        valid_n = c_io < t_row
        cn = _cum_dot(jnp.where(valid_n, _softplus2(zn), 0.0), _tri_incl(LANES))
        wn = jnp.where(valid_n, jnp.exp2(zn - cn), 0.0)
        run = cn[:, 0:1]
        tk = CUMSUM_TILE
        tri = _tri_incl(tk)
        w_parts = [None] * (past // tk)
        for j in reversed(range(past // tk)):
            zb = z[:, j * tk:(j + 1) * tk]
            c = _cum_dot(_softplus2(zb), tri)
            w_parts[j] = jnp.exp2(zb - c - run)
            run = run + c[:, 0:1]
        w = jnp.concatenate(w_parts, axis=1)
        denom = None
    else:
        lfn = lfn_ref[0]
        nh = n_kv * g
        tri = _tri_strict(page)
        bias_n = jnp.dot(lfn, tri, preferred_element_type=F32, precision=lax.Precision.HIGHEST)
        carry = jnp.sum(lfn, axis=-1, keepdims=True)
        bias_parts = [None] * n_pages
        for p in reversed(range(n_pages)):
            lfp = lfbuf[slot, p]
            bias_parts[p] = jnp.dot(lfp, tri, preferred_element_type=F32,
                                    precision=lax.Precision.HIGHEST) + carry
            carry = carry + jnp.sum(lfp, axis=-1, keepdims=True)
        bias_h = jnp.concatenate(bias_parts, axis=1)
        bias = jnp.concatenate([jnp.broadcast_to(bias_h[i:i + 1], (t_new, past)) for i in range(nh)], axis=0)
        bias_new = jnp.concatenate([jnp.broadcast_to(bias_n[i:i + 1], (t_new, LANES)) for i in range(nh)], axis=0)
        s = z + bias * LOG2E
        sn = jnp.where(c_io <= t_row, zn + bias_new * LOG2E, NEG_BIG)
        m = jnp.maximum(jnp.max(s, axis=-1, keepdims=True), jnp.max(sn, axis=-1, keepdims=True))
        w = jnp.exp2(s - m)
        wn = jnp.exp2(sn - m)
        denom = jnp.sum(w, axis=-1, keepdims=True) + jnp.sum(wn, axis=-1, keepdims=True)

    wb = w.astype(BF16)
    wnb = wn.astype(BF16)
    o_cols = []
    rh = g * t_new
    for h in range(n_kv):
        vh = vbuf[slot, :, pl.ds(h, page, stride=n_kv), :].reshape(past, hd).astype(BF16)
        vnh = vn[:, h * hd:(h + 1) * hd].astype(BF16)
        oh = (jnp.dot(wb[h * rh:(h + 1) * rh], vh, preferred_element_type=F32)
              + jnp.dot(wnb[h * rh:(h + 1) * rh], vnh, preferred_element_type=F32))
        if fox:
            oh = oh / denom[h * rh:(h + 1) * rh]
        for i in range(g):
            o_cols.append(oh[i * t_new:(i + 1) * t_new])
    o_ref[0] = jnp.concatenate(o_cols, axis=1).astype(BF16)


def _sample_attention(page_table, q_b, k_new, v_new, k_pool, v_pool, lf_new_t, lf_pool_t, *, n_kv, g, hd):
    nb, t_new, dq = q_b.shape
    n_pages = page_table.shape[1]
    page = k_pool.shape[1] // n_kv
    nkv = n_kv * hd
    fox = lf_pool_t is not None
    in_specs = [pl.BlockSpec((1, t_new, dq), lambda b, pt: (b, 0, 0)),
                pl.BlockSpec((1, t_new, nkv), lambda b, pt: (b, 0, 0)),
                pl.BlockSpec((1, t_new, nkv), lambda b, pt: (b, 0, 0))]
    args = [q_b, k_new, v_new]
    scratch = [pltpu.VMEM((2, n_pages, page * n_kv, hd), F32),
               pltpu.VMEM((2, n_pages, page * n_kv, hd), F32)]
    if fox:
        nh = n_kv * g
        in_specs.append(pl.BlockSpec((1, nh, LANES), lambda b, pt: (b, 0, 0)))
        args.append(lf_new_t)
    in_specs += [pl.BlockSpec(memory_space=pl.ANY), pl.BlockSpec(memory_space=pl.ANY)]
    args += [k_pool, v_pool]
    if fox:
        in_specs.append(pl.BlockSpec(memory_space=pl.ANY))
        args.append(lf_pool_t)
        scratch.append(pltpu.VMEM((2, n_pages, nh, page), F32))
    scratch.append(pltpu.SemaphoreType.DMA((3, 2)))
    return pl.pallas_call(
        functools.partial(_sample_att_body, n_pages=n_pages, page=page, n_kv=n_kv, g=g, hd=hd,
                          t_new=t_new, fox=fox),
        grid_spec=pltpu.PrefetchScalarGridSpec(
            num_scalar_prefetch=1, grid=(nb,), in_specs=in_specs,
            out_specs=pl.BlockSpec((1, t_new, dq), lambda b, pt: (b, 0, 0)),
            scratch_shapes=scratch),
        out_shape=jax.ShapeDtypeStruct((nb, t_new, dq), BF16),
        compiler_params=_params("arbitrary"),
        name="fox_sample_attention" if fox else "sb_sample_attention",
    )(page_table, *args)


def _layernorm(u, gamma, beta):
    mu = jnp.mean(u, axis=-1, keepdims=True)
    uc = u - mu
    var = jnp.mean(uc * uc, axis=-1, keepdims=True)
    return uc * lax.rsqrt(var + LN_EPS) * gamma + beta


def _first_index_of_max(vals, mask, lane):
    top = jnp.max(jnp.where(mask, vals, NEG_BIG), axis=-1, keepdims=True)
    idx = jnp.min(jnp.where(mask & (vals == top), lane, 2 * LANES), axis=-1, keepdims=True)
    return top, idx


def _post_attn_body(op_ref, os_ref, xp_ref, xs_ref, g1_ref, sh2_ref, sc2_ref, w_ref, lng_ref, lnb_ref, rw_ref,
                    rb_ref, x1_ref, h2_ref, eid_ref, gate_ref, *, alpha, n_exp, n_prompt_tiles):
    is_prompt = pl.program_id(0) < n_prompt_tiles
    o = jnp.where(is_prompt, op_ref[...], os_ref[...])
    x = jnp.where(is_prompt, xp_ref[...], xs_ref[...])
    y = jnp.dot(o, w_ref[...], preferred_element_type=F32)
    x1 = _layernorm(alpha * x + g1_ref[0] * y, lng_ref[...], lnb_ref[...])
    x1_ref[...] = x1
    h2 = x1 * (1.0 + sc2_ref[0]) + sh2_ref[0]
    h2_ref[...] = h2

    logits = _dot_split(h2, rw_ref[...])
    lane = lax.broadcasted_iota(jnp.int32, logits.shape, 1)
    real = lane < n_exp
    logits = jnp.where(real, logits, NEG_BIG)
    e = jnp.exp(logits - jnp.max(logits, axis=-1, keepdims=True))
    probs = e / jnp.sum(e, axis=-1, keepdims=True)
    sel = probs + rb_ref[...]
    epg = n_exp // N_GROUPS
    best = first = second = None
    for grp in range(N_GROUPS):
        in_g = (lane >= grp * epg) & (lane < (grp + 1) * epg)
        top1, idx1 = _first_index_of_max(sel, in_g, lane)
        top2, idx2 = _first_index_of_max(sel, in_g & (lane != idx1), lane)
        score = top1 + top2
        if grp == 0:
            best, first, second = score, idx1, idx2
        else:
            better = score > best
            best = jnp.where(better, score, best)
            first = jnp.where(better, idx1, first)
            second = jnp.where(better, idx2, second)
    gate1 = jnp.sum(jnp.where(lane == first, probs, 0.0), axis=-1, keepdims=True)
    gate2 = jnp.sum(jnp.where(lane == second, probs, 0.0), axis=-1, keepdims=True)
    tot = gate1 + gate2
    eid_ref[...] = jnp.where(lane == 0, first, jnp.where(lane == 1, second, 0))
    gate_ref[...] = jnp.where(lane == 0, gate1 / tot, jnp.where(lane == 1, gate2 / tot, 0.0))


def _post_attention(o_p, o_s, x_p, x_s, mod, layer, w_out, ln_g, ln_b, rw_pad, rb_pad, *, alpha, n_exp):
    n_p, d = x_p.shape
    n = n_p + x_s.shape[0]
    tm = TOKEN_TILE
    n_prompt_tiles = n_p // tm
    kind = lambda i: i // n_prompt_tiles
    row = pl.BlockSpec((tm, d), lambda i: (i, 0))
    row_p = _prompt_rows(tm, d, n_prompt_tiles)
    row_s = _sample_rows(tm, d, n_prompt_tiles)
    modspec = lambda c: pl.BlockSpec((1, tm, d), lambda i: (layer, kind(i), c))
    const = lambda a: pl.BlockSpec(a.shape, lambda i: (0,) * a.ndim)
    lanes = pl.BlockSpec((tm, LANES), lambda i: (i, 0))
    return pl.pallas_call(
        functools.partial(_post_attn_body, alpha=alpha, n_exp=n_exp, n_prompt_tiles=n_prompt_tiles),
        grid=(n // tm,),
        in_specs=[row_p, row_s, row_p, row_s, modspec(2), modspec(3), modspec(4), const(w_out), const(ln_g),
                  const(ln_b), const(rw_pad), const(rb_pad)],
        out_specs=[row, row, lanes, lanes],
        out_shape=[jax.ShapeDtypeStruct((n, d), F32), jax.ShapeDtypeStruct((n, d), F32),
                   jax.ShapeDtypeStruct((n, LANES), jnp.int32), jax.ShapeDtypeStruct((n, LANES), F32)],
        compiler_params=_params("arbitrary"),
        name="post_attention",
    )(o_p, o_s, x_p, x_s, mod, mod, mod, w_out, ln_g, ln_b, rw_pad, rb_pad)


def _moe_body(texp_ref, trows_ref, src_ref, dst_ref, srcn_ref, gate_ref, wg_ref, wu_ref, wd_ref,
              h_hbm, out_hbm, xbuf, ybuf, sem, *, tm):
    t = pl.program_id(0)
    nt = pl.num_programs(0)
    slot = t % 2
    del texp_ref
    rows_of = lambda i: trows_ref[jnp.clip(i, 0, nt - 1)]

    def start_gather(idx_ref, sl):
        def chunk(c, _):
            for u in range(ROW_DMA_UNROLL):
                r = c * ROW_DMA_UNROLL + u
                pltpu.make_async_copy(h_hbm.at[idx_ref[0, 0, r]], xbuf.at[sl, r], sem.at[0, sl]).start()
            return 0
        lax.fori_loop(0, tm // ROW_DMA_UNROLL, chunk, 0)

    def wait_gather(sl):
        pltpu.make_async_copy(h_hbm.at[pl.ds(0, tm)], xbuf.at[sl], sem.at[0, sl]).wait()

    def start_scatter(sl, n_rows):
        def one(r):
            pltpu.make_async_copy(ybuf.at[sl, r], out_hbm.at[dst_ref[0, 0, r]], sem.at[1, sl]).start()

        def chunk(c, _):
            for u in range(ROW_DMA_UNROLL):
                one(c * ROW_DMA_UNROLL + u)
            return 0
        n_full = n_rows // ROW_DMA_UNROLL
        lax.fori_loop(0, n_full, chunk, 0)
        lax.fori_loop(n_full * ROW_DMA_UNROLL, n_rows, lambda r, _: (one(r), 0)[1], 0)

    def wait_scatter(sl, n_rows):
        p = tm
        while p >= 1:
            @pl.when((n_rows & p) != 0)
            def _(p=p):
                pltpu.make_async_copy(ybuf.at[sl, pl.ds(0, p)], out_hbm.at[pl.ds(0, p)], sem.at[1, sl]).wait()
            p //= 2

    @pl.when((t == 0) & (rows_of(0) > 0))
    def _():
        start_gather(src_ref, 0)

    @pl.when((t + 1 < nt) & (rows_of(t + 1) > 0))
    def _():
        start_gather(srcn_ref, 1 - slot)

    @pl.when(t >= 2)
    def _():
        wait_scatter(slot, rows_of(t - 2))

    @pl.when(rows_of(t) > 0)
    def _():
        wait_gather(slot)
        xb = xbuf[slot].astype(BF16)
        gp = jnp.dot(xb, wg_ref[0, 0].astype(BF16), preferred_element_type=F32)
        up = jnp.dot(xb, wu_ref[0, 0].astype(BF16), preferred_element_type=F32)
        act = gp / (1.0 + jnp.exp(-gp)) * up * gate_ref[...]
        ybuf[slot] = jnp.dot(act.astype(BF16), wd_ref[0, 0].astype(BF16), preferred_element_type=F32)
        start_scatter(slot, rows_of(t))

    @pl.when(t == nt - 1)
    def _():
        @pl.when(nt >= 2)
        def _():
            wait_scatter(1 - slot, rows_of(t - 1))
        wait_scatter(slot, rows_of(t))


def _moe(h2, tile_expert, tile_rows, row_src, row_dst, row_gate, w_gate, w_up, w_down, layer):
    n, d = h2.shape
    tm = MOE_ROW_TILE
    n_tiles = tile_expert.shape[0]
    f = w_gate.shape[-1]
    nxt = lambda t: jnp.minimum(t + 1, n_tiles - 1)
    smem_rows = lambda imap: pl.BlockSpec((1, 1, tm), imap, memory_space=pltpu.SMEM)
    return pl.pallas_call(
        functools.partial(_moe_body, tm=tm),
        grid_spec=pltpu.PrefetchScalarGridSpec(
            num_scalar_prefetch=2, grid=(n_tiles,),
            in_specs=[smem_rows(lambda t, te, tr: (t, 0, 0)),
                      smem_rows(lambda t, te, tr: (t, 0, 0)),
                      smem_rows(lambda t, te, tr: (nxt(t), 0, 0)),
                      pl.BlockSpec((tm, 1), lambda t, te, tr: (t, 0)),
                      pl.BlockSpec((1, 1, d, f), lambda t, te, tr: (layer, te[t], 0, 0)),
                      pl.BlockSpec((1, 1, d, f), lambda t, te, tr: (layer, te[t], 0, 0)),
                      pl.BlockSpec((1, 1, f, d), lambda t, te, tr: (layer, te[t], 0, 0)),
                      pl.BlockSpec(memory_space=pl.ANY)],
            out_specs=pl.BlockSpec(memory_space=pl.ANY),
            scratch_shapes=[pltpu.VMEM((2, tm, d), F32), pltpu.VMEM((2, tm, d), F32),
                            pltpu.SemaphoreType.DMA((2, 2))]),
        out_shape=jax.ShapeDtypeStruct((TOP_K * n, d), F32),
        compiler_params=_params("arbitrary"),
        name="sparse_experts",
    )(tile_expert, tile_rows, row_src, row_dst, row_src, row_gate, w_gate, w_up, w_down, h2)


def _routing_tables(eid, gate, n_exp, tm):
    n = eid.shape[0]
    n_assign = n * TOP_K
    n_tiles = (n_assign + n_exp * (tm - 1)) // tm + 1
    e_flat = eid.reshape(n_assign)
    order = jnp.argsort(e_flat, stable=True).astype(jnp.int32)
    counts = jnp.sum((e_flat[:, None] == jnp.arange(n_exp, dtype=jnp.int32)[None, :]).astype(jnp.int32), axis=0)
    starts = jnp.cumsum(counts) - counts
    tiles_per = (counts + tm - 1) // tm
    tile_starts = jnp.cumsum(tiles_per) - tiles_per
    n_used = jnp.sum(tiles_per).astype(jnp.int32)
    tile_ids = jnp.arange(n_tiles, dtype=jnp.int32)
    tile_ends = tile_starts + tiles_per
    tile_expert = jnp.sum((tile_ids[:, None] >= tile_ends[None, :]).astype(jnp.int32), axis=1)
    tile_expert = jnp.minimum(tile_expert, n_exp - 1)
    row_in_tile = jnp.arange(tm, dtype=jnp.int32)[None, :]
    within = (tile_ids - tile_starts[tile_expert])[:, None] * tm + row_in_tile
    valid = (within < counts[tile_expert][:, None]) & (tile_ids[:, None] < n_used)
    sorted_pos = jnp.clip(starts[tile_expert][:, None] + within, 0, n_assign - 1)
    assign = order[sorted_pos]
    tok = assign // TOP_K
    slot_k = assign % TOP_K
    row_src = jnp.where(valid, tok, 0).astype(jnp.int32)
    row_dst = jnp.where(valid, slot_k * n + tok, 0).astype(jnp.int32)
    row_gate = jnp.where(valid, gate.reshape(n_assign)[assign], 0.0).astype(F32)
    tile_rows = jnp.sum(valid.astype(jnp.int32), axis=1)
    return (tile_expert, tile_rows, row_src.reshape(n_tiles, 1, tm),
            row_dst.reshape(n_tiles, 1, tm), row_gate.reshape(n_tiles * tm, 1))


def _combine_body(x1_ref, ya_ref, yb_ref, g2_ref, lng_ref, lnb_ref, op_ref, os_ref, *, alpha, n_prompt_tiles,
                  nb, seq_major):
    i = pl.program_id(0)
    y = ya_ref[...] + yb_ref[...]
    out = _layernorm(alpha * x1_ref[...] + g2_ref[0] * y, lng_ref[...], lnb_ref[...])

    @pl.when(i < n_prompt_tiles)
    def _():
        op_ref[...] = out

    @pl.when(i >= n_prompt_tiles)
    def _():
        if seq_major:
            d = out.shape[1]
            for t in range(out.shape[0] // nb):
                os_ref[:, t * d:(t + 1) * d] = out[t * nb:(t + 1) * nb]
        else:
            os_ref[...] = out


def _combine(x1, y2, mod, layer, ln_g, ln_b, *, n_prompt_tiles, nb, alpha, seq_major):
    n, d = x1.shape
    tm = TOKEN_TILE
    n_p = n_prompt_tiles * tm
    n_s = n - n_p
    kind = lambda i: i // n_prompt_tiles
    row = pl.BlockSpec((tm, d), lambda i: (i, 0))
    const = lambda a: pl.BlockSpec(a.shape, lambda i: (0,) * a.ndim)
    if seq_major:
        s_spec = pl.BlockSpec((nb, (tm // nb) * d), lambda i: (0, jnp.maximum(i - n_prompt_tiles, 0)))
        s_shape = jax.ShapeDtypeStruct((nb, (n_s // nb) * d), F32)
    else:
        s_spec = _sample_rows(tm, d, n_prompt_tiles)
        s_shape = jax.ShapeDtypeStruct((n_s, d), F32)
    return pl.pallas_call(
        functools.partial(_combine_body, alpha=alpha, n_prompt_tiles=n_prompt_tiles, nb=nb, seq_major=seq_major),
        grid=(n // tm,),
        in_specs=[row, row, pl.BlockSpec((tm, d), lambda i: (n // tm + i, 0)),
                  pl.BlockSpec((1, tm, d), lambda i: (layer, kind(i), 5)), const(ln_g), const(ln_b)],
        out_specs=[_prompt_rows(tm, d, n_prompt_tiles), s_spec],
        out_shape=[jax.ShapeDtypeStruct((n_p, d), F32), s_shape],
        compiler_params=_params("arbitrary"),
        name="combine_layernorm",
    )(x1, y2, y2, mod, ln_g, ln_b)


def kernel(x_prompt, x_sample, cache_k_sb, cache_v_sb, cache_k_fox, cache_v_fox, cache_logf_fox, page_table,
           c_prompt, c_sample, ada_w, ada_b, ln_g, ln_b, w_in_sb, w_out_sb, w_in_fox, b_f_fox, w_out_fox,
           router_w, router_b, w_gate, w_up, w_down):
    batch, seq, d = x_prompt.shape
    nb, t_new, _ = x_sample.shape
    depth = ada_w.shape[0]
    n_pool, page, n_kv, hd = cache_k_sb.shape[1:]
    n_heads = d // hd
    g = n_heads // n_kv
    nq, nkv = n_heads * hd, n_kv * hd
    n_exp = router_w.shape[1]
    scale = float(hd) ** -0.5 * LOG2E
    alpha = (2.0 * depth) ** 0.25
    tm = TOKEN_TILE
    n_p, n_s = batch * seq, nb * t_new
    assert batch == 1 and n_p % tm == 0 and n_s % tm == 0 and tm % nb == 0
    assert seq % ATT_KV_TILE == 0 and seq % FOX_KV_TILE == 0 and seq % ATT_Q_TILE == 0
    assert ATT_KV_TILE % ATT_Q_TILE == 0 and FOX_KV_TILE % ATT_Q_TILE == 0 and ATT_KV_TILE % CUMSUM_TILE == 0
    assert (page * page_table.shape[1]) % CUMSUM_TILE == 0
    assert n_exp % N_GROUPS == 0 and n_exp <= LANES and n_heads <= LANES and t_new <= LANES
    n_prompt_tiles = n_p // tm

    x_p = x_prompt.reshape(n_p, d)
    x_s = jnp.transpose(x_sample, (1, 0, 2)).reshape(n_s, d)
    c_rows = jnp.concatenate([jnp.broadcast_to(c_prompt, (tm, d))] + [c_sample] * (tm // nb), axis=0)
    mod = _modulation(c_rows, ada_w, ada_b)

    rw_pad = _hi_lo_columns(router_w)
    rb_pad = jnp.pad(router_b.reshape(1, n_exp), ((0, 0), (0, LANES - n_exp)))

    def to_pos_major(a):
        return jnp.transpose(a, (1, 0, 2)).reshape(n_s, a.shape[-1])

    sb_out, fox_out = [], []
    for layer in range(depth):
        j = layer // 2
        fox = layer % 2 == 1
        if fox:
            w_in, w_out = w_in_fox[j], w_out_fox[j]
            w_main = w_in[:, :nq + 2 * nkv].astype(BF16)
            w_f = _hi_lo_columns(w_in[:, nq + 2 * nkv:])
            b_f = jnp.pad(b_f_fox[j].reshape(1, n_heads), ((0, 0), (0, LANES - n_heads)))
            k_pool, v_pool = cache_k_fox[j], cache_v_fox[j]
        else:
            w_in, w_out = w_in_sb[j], w_out_sb[j]
            w_main, w_f, b_f = w_in.astype(BF16), None, None
            k_pool, v_pool = cache_k_sb[j], cache_v_sb[j]
        res = _qkv_proj(x_p, x_s, mod, layer, w_main, w_f, b_f, nb=nb, nq=nq, nkv=nkv, hd=hd, scale=scale)
        q_p, k_p, v_p, kb_p, vb_p = res[:5]
        q_s = res[5].reshape(nb, t_new, nq)
        k_s, v_s = res[6].reshape(nb, t_new, nkv), res[7].reshape(nb, t_new, nkv)
        k_pool = k_pool.reshape(n_pool, page * n_kv, hd)
        v_pool = v_pool.reshape(n_pool, page * n_kv, hd)
        kv_out = (k_p.reshape(batch, seq, n_kv, hd), v_p.reshape(batch, seq, n_kv, hd),
                  k_s.reshape(nb, t_new, n_kv, hd), v_s.reshape(nb, t_new, n_kv, hd))
        if fox:
            lf_p = res[8][:, :n_heads]
            lf_s = res[9].reshape(nb, t_new, LANES)[:, :, :n_heads]
            ck_t = _cumsum_lanes(jnp.transpose(lf_p))
            lf_new_t = jnp.pad(jnp.transpose(lf_s, (0, 2, 1)), ((0, 0), (0, 0), (0, LANES - t_new)))
            lf_pool_t = jnp.transpose(cache_logf_fox[j], (0, 2, 1))
            o_p = _prompt_attention(q_p, kb_p, vb_p, ck_t, seq=seq, n_kv=n_kv, g=g, hd=hd)
            o_s = _sample_attention(page_table, q_s, k_s, v_s, k_pool, v_pool, lf_new_t, lf_pool_t,
                                    n_kv=n_kv, g=g, hd=hd)
            fox_out.append(kv_out[:2] + (lf_p.reshape(batch, seq, n_heads),) + kv_out[2:] + (lf_s,))
        else:
            o_p = _prompt_attention(q_p, kb_p, vb_p, None, seq=seq, n_kv=n_kv, g=g, hd=hd)
            o_s = _sample_attention(page_table, q_s, k_s, v_s, k_pool, v_pool, None, None,
                                    n_kv=n_kv, g=g, hd=hd)
            sb_out.append(kv_out)
        x1, h2, eid, gate = _post_attention(
            o_p, to_pos_major(o_s), x_p, x_s, mod, layer, w_out.astype(BF16),
            ln_g[layer, 0].reshape(1, d), ln_b[layer, 0].reshape(1, d), rw_pad, rb_pad, alpha=alpha, n_exp=n_exp)
        tables = _routing_tables(eid[:, :TOP_K], gate[:, :TOP_K], n_exp, MOE_ROW_TILE)
        y2 = _moe(h2, *tables, w_gate, w_up, w_down, layer)
        x_p, x_s = _combine(x1, y2, mod, layer, ln_g[layer, 1].reshape(1, d), ln_b[layer, 1].reshape(1, d),
                            n_prompt_tiles=n_prompt_tiles, nb=nb, alpha=alpha, seq_major=layer == depth - 1)

    y_prompt = x_p.reshape(batch, seq, d)
    y_sample = x_s.reshape(nb, t_new, d)
    stack = lambda items, i: jnp.stack([e[i] for e in items])
    return (y_prompt, y_sample,
            stack(sb_out, 0), stack(sb_out, 1), stack(sb_out, 2), stack(sb_out, 3),
            stack(fox_out, 0), stack(fox_out, 1), stack(fox_out, 2),
            stack(fox_out, 3), stack(fox_out, 4), stack(fox_out, 5))
```

```python
import functools

import jax
import jax.numpy as jnp
from jax import lax
from jax.experimental import pallas as pl
from jax.experimental.pallas import tpu as pltpu

F32 = jnp.float32
BF16 = jnp.bfloat16

N_GROUPS = 4
TOP_K = 2
LN_EPS = 1e-5
NEG_BIG = -1e30

TOKEN_TILE = 256
MOE_ROW_TILE = 256
ROW_DMA_UNROLL = 8
ATT_Q_TILE = 256
CUMSUM_TILE = 256
ATT_KV_TILE = 512
FOX_KV_TILE = 1024
LANES = 128
V7X_VMEM_LIMIT = 56 * 1024 * 1024


def _params(*sem):
    return pltpu.CompilerParams(dimension_semantics=sem, vmem_limit_bytes=V7X_VMEM_LIMIT)


LOG2E = 1.4426950408889634


def _log_sigmoid_neg(z):
    return -(jnp.maximum(z, 0.0) + jnp.log(1.0 + jnp.exp(-jnp.abs(z))))


def _softplus2(z2):
    neg_abs = pltpu.bitcast(pltpu.bitcast(z2, jnp.uint32) | jnp.uint32(0x80000000), F32)
    return jnp.maximum(z2, 0.0) + jnp.log2(1.0 + jnp.exp2(neg_abs))


def _cum_dot(a, tri):
    return jnp.dot(a.astype(BF16), tri, preferred_element_type=F32)


def _hi_lo_columns(w):
    w = jnp.pad(w, ((0, 0), (0, LANES - w.shape[1])))
    hi = w.astype(BF16)
    lo = (w - hi.astype(F32)).astype(BF16)
    return jnp.concatenate([hi, lo], axis=1)


def _dot_split(a, w_hl):
    hi = a.astype(BF16)
    lo = (a - hi.astype(F32)).astype(BF16)
    p = jnp.dot(hi, w_hl, preferred_element_type=F32)
    q = jnp.dot(lo, w_hl[:, :LANES], preferred_element_type=F32)
    return p[:, :LANES] + p[:, LANES:] + q


def _tri_incl(n):
    r = lax.broadcasted_iota(jnp.int32, (n, n), 0)
    c = lax.broadcasted_iota(jnp.int32, (n, n), 1)
    return (r >= c).astype(BF16)


def _tri_strict(n):
    r = lax.broadcasted_iota(jnp.int32, (n, n), 0)
    c = lax.broadcasted_iota(jnp.int32, (n, n), 1)
    return (r > c).astype(F32)


def _mod_body(c_ref, w_ref, b_ref, o_ref):
    c = c_ref[...]
    a = c / (1.0 + jnp.exp(-c))
    o_ref[0] = jnp.dot(a.astype(BF16), w_ref[0].astype(BF16),
                       preferred_element_type=F32) + b_ref[0]


def _modulation(c_rows, ada_w, ada_b):
    depth, d, d6 = ada_w.shape
    rows = c_rows.shape[0]
    tn = min(1024, d6)
    return pl.pallas_call(
        _mod_body,
        grid=(depth, d6 // tn),
        in_specs=[pl.BlockSpec((rows, d), lambda l, j: (0, 0)),
                  pl.BlockSpec((1, d, tn), lambda l, j: (l, 0, j)),
                  pl.BlockSpec((1, 1, tn), lambda l, j: (l, 0, j))],
        out_specs=pl.BlockSpec((1, rows, tn), lambda l, j: (l, 0, j)),
        out_shape=jax.ShapeDtypeStruct((depth, rows, d6), F32),
        compiler_params=_params("arbitrary", "arbitrary"),
        name="modulation",
    )(c_rows, ada_w, ada_b.reshape(depth, 1, d6))


def _prompt_rows(tm, width, n_prompt_tiles):
    return pl.BlockSpec((tm, width), lambda i: (jnp.minimum(i, n_prompt_tiles - 1), 0))


def _sample_rows(tm, width, n_prompt_tiles):
    return pl.BlockSpec((tm, width), lambda i: (jnp.maximum(i - n_prompt_tiles, 0), 0))


def _qkv_body(*refs, nq, nkv, scale, fox, n_prompt_tiles, nb):
    if fox:
        (xp_ref, xs_ref, sh_ref, sc_ref, w_ref, wf_ref, bf_ref, qp_ref, kp_ref, vp_ref, kbp_ref, vbp_ref,
         qs_ref, ks_ref, vs_ref, lfp_ref, lfs_ref) = refs
    else:
        (xp_ref, xs_ref, sh_ref, sc_ref, w_ref, qp_ref, kp_ref, vp_ref, kbp_ref, vbp_ref,
         qs_ref, ks_ref, vs_ref) = refs
    i = pl.program_id(0)
    x = jnp.where(i < n_prompt_tiles, xp_ref[...], xs_ref[...])
    h = x * (1.0 + sc_ref[0]) + sh_ref[0]
    p = jnp.dot(h.astype(BF16), w_ref[...], preferred_element_type=F32)
    q = (p[:, :nq] * scale).astype(BF16)
    k = p[:, nq:nq + nkv]
    v = p[:, nq + nkv:nq + 2 * nkv]
    if fox:
        lf = _log_sigmoid_neg(-(_dot_split(h, wf_ref[...]) + bf_ref[...]))

    @pl.when(i < n_prompt_tiles)
    def _():
        qp_ref[...] = q
        n_kv = nkv // kp_ref.shape[1]
        for hh in range(n_kv):
            cols = slice(hh * kp_ref.shape[1], (hh + 1) * kp_ref.shape[1])
            kp_ref[pl.ds(hh, k.shape[0], stride=n_kv), :] = k[:, cols]
            vp_ref[pl.ds(hh, v.shape[0], stride=n_kv), :] = v[:, cols]
        kbp_ref[...] = k.astype(BF16)
        vbp_ref[...] = v.astype(BF16)
        if fox:
            lfp_ref[...] = lf

    @pl.when(i >= n_prompt_tiles)
    def _():
        for t in range(q.shape[0] // nb):
            rows = slice(t * nb, (t + 1) * nb)
            qs_ref[:, t * nq:(t + 1) * nq] = q[rows]
            ks_ref[:, t * nkv:(t + 1) * nkv] = k[rows]
            vs_ref[:, t * nkv:(t + 1) * nkv] = v[rows]
            if fox:
                lfs_ref[:, t * LANES:(t + 1) * LANES] = lf[rows]


def _qkv_proj(x_p, x_s, mod, layer, w_main, w_f, b_f, *, nb, nq, nkv, hd, scale):
    n_p, d = x_p.shape
    n_s = x_s.shape[0]
    tm = TOKEN_TILE
    fox = w_f is not None
    n_prompt_tiles = n_p // tm
    n = n_p + n_s
    t_new = n_s // nb
    per = tm // nb
    kind = lambda i: i // n_prompt_tiles
    in_specs = [_prompt_rows(tm, d, n_prompt_tiles), _sample_rows(tm, d, n_prompt_tiles),
                pl.BlockSpec((1, tm, d), lambda i: (layer, kind(i), 0)),
                pl.BlockSpec((1, tm, d), lambda i: (layer, kind(i), 1)),
                pl.BlockSpec(w_main.shape, lambda i: (0, 0))]
    args = [x_p, x_s, mod, mod, w_main]
    if fox:
        in_specs += [pl.BlockSpec(w_f.shape, lambda i: (0, 0)),
                     pl.BlockSpec(b_f.shape, lambda i: (0, 0))]
        args += [w_f, b_f]
    n_kv = nkv // hd
    p_spec = lambda w: _prompt_rows(tm, w, n_prompt_tiles)
    kv_spec = _prompt_rows(tm * n_kv, hd, n_prompt_tiles)
    s_spec = lambda w: pl.BlockSpec((nb, per * w), lambda i: (0, jnp.maximum(i - n_prompt_tiles, 0)))
    p_shape = lambda w, dt: jax.ShapeDtypeStruct((n_p, w), dt)
    kv_shape = jax.ShapeDtypeStruct((n_p * n_kv, hd), F32)
    s_shape = lambda w, dt: jax.ShapeDtypeStruct((nb, t_new * w), dt)
    out_specs = [p_spec(nq), kv_spec, kv_spec, p_spec(nkv), p_spec(nkv),
                 s_spec(nq), s_spec(nkv), s_spec(nkv)]
    out_shape = [p_shape(nq, BF16), kv_shape, kv_shape, p_shape(nkv, BF16), p_shape(nkv, BF16),
                 s_shape(nq, BF16), s_shape(nkv, F32), s_shape(nkv, F32)]
    if fox:
        out_specs += [p_spec(LANES), s_spec(LANES)]
        out_shape += [p_shape(LANES, F32), s_shape(LANES, F32)]
    return pl.pallas_call(
        functools.partial(_qkv_body, nq=nq, nkv=nkv, scale=scale, fox=fox,
                          n_prompt_tiles=n_prompt_tiles, nb=nb),
        grid=(n // tm,),
        in_specs=in_specs, out_specs=out_specs, out_shape=out_shape,
        compiler_params=_params("arbitrary"),
        name="qkv_fox" if fox else "qkv_sb",
    )(*args)


def _stack_heads(q, g, hd):
    return jnp.concatenate([q[:, i * hd:(i + 1) * hd] for i in range(g)], axis=0)


def _unstack_heads(o, g, tq):
    return jnp.concatenate([o[i * tq:(i + 1) * tq] for i in range(g)], axis=1)


def _causal_sweep(qi, tq, tk, carry, block):
    jd = (qi * tq) // tk
    carry = block(jd * tk, tk, carry, qi * tq - jd * tk)

    def pair(i, c):
        j = jd - 1 - 2 * i
        return block((j - 1) * tk, tk, block(j * tk, tk, c, None), None)
    carry = lax.fori_loop(0, jd // 2, pair, carry)
    return lax.cond(jd % 2 == 1, lambda c: block(0, tk, c, None), lambda c: c, carry)


def _sb_prompt_body(q_ref, k_ref, v_ref, o_ref, *, tq, tk, g, hd):
    qi = pl.program_id(1)
    rows = g * tq
    qs = _stack_heads(q_ref[...], g, hd)
    tc = CUMSUM_TILE
    tri = _tri_incl(tc)

    def block(start, width, carry, diag_off):
        run, acc = carry
        masked = diag_off is not None
        start = start if isinstance(start, int) else pl.multiple_of(start, tq)
        kb = k_ref[pl.ds(start, width), :]
        vb = v_ref[pl.ds(start, width), :]
        z = lax.dot_general(qs, kb, (((1,), (1,)), ((), ())), preferred_element_type=F32)
        sp = _softplus2(z)
        if masked:
            r_io = lax.broadcasted_iota(jnp.int32, (rows, width), 0)
            c_io = lax.broadcasted_iota(jnp.int32, (rows, width), 1)
            valid = c_io < diag_off + (r_io & (tq - 1))
            sp = jnp.where(valid, sp, 0.0)
        w_parts = [None] * (width // tc)
        for i in reversed(range(width // tc)):
            c = _cum_dot(sp[:, i * tc:(i + 1) * tc], tri)
            w_parts[i] = jnp.exp2(z[:, i * tc:(i + 1) * tc] - c - run)
            run = run + c[:, 0:1]
        w = jnp.concatenate(w_parts, axis=1)
        if masked:
            w = jnp.where(valid, w, 0.0)
        acc = acc + jnp.dot(w.astype(BF16), vb, preferred_element_type=F32)
        return run, acc

    carry = (jnp.zeros((rows, 1), F32), jnp.zeros((rows, hd), F32))
    _, acc = _causal_sweep(qi, tq, tk, carry, block)
    o_ref[...] = _unstack_heads(acc, g, tq).astype(BF16)


def _fox_prompt_body(q_ref, k_ref, v_ref, ck_ref, o_ref, *, tq, tk, g, hd):
    h = pl.program_id(0)
    qi = pl.program_id(1)
    rows = g * tq
    qs = _stack_heads(q_ref[...], g, hd)

    def block(start, width, carry, diag_off):
        m, l, acc = carry
        start = start if isinstance(start, int) else pl.multiple_of(start, tq)
        kb = k_ref[pl.ds(start, width), :]
        vb = v_ref[pl.ds(start, width), :]
        z = lax.dot_general(qs, kb, (((1,), (1,)), ((), ())), preferred_element_type=F32)
        s = jnp.concatenate(
            [z[i * tq:(i + 1) * tq] - ck_ref[pl.ds(h * g + i, 1), pl.ds(start, width)] for i in range(g)], axis=0)
        if diag_off is not None:
            r_io = lax.broadcasted_iota(jnp.int32, (rows, width), 0)
            c_io = lax.broadcasted_iota(jnp.int32, (rows, width), 1)
            s = jnp.where(c_io <= diag_off + (r_io & (tq - 1)), s, NEG_BIG)
        m_new = jnp.maximum(m, jnp.max(s, axis=-1, keepdims=True))
        alpha = jnp.exp2(m - m_new)
        p = jnp.exp2(s - m_new)
        l = alpha * l + jnp.sum(p, axis=-1, keepdims=True)
        acc = alpha * acc + jnp.dot(p.astype(BF16), vb, preferred_element_type=F32)
        return m_new, l, acc

    carry = (jnp.full((rows, 1), NEG_BIG, F32), jnp.zeros((rows, 1), F32), jnp.zeros((rows, hd), F32))
    _, l, acc = _causal_sweep(qi, tq, tk, carry, block)
    o_ref[...] = _unstack_heads(acc / l, g, tq).astype(BF16)


def _prompt_attention(q, kb, vb, ck_t, *, seq, n_kv, g, hd):
    fox = ck_t is not None
    tq, tk = ATT_Q_TILE, (FOX_KV_TILE if fox else ATT_KV_TILE)
    in_specs = [pl.BlockSpec((tq, g * hd), lambda h, i: (i, h)),
                pl.BlockSpec((seq, hd), lambda h, i: (0, h)),
                pl.BlockSpec((seq, hd), lambda h, i: (0, h))]
    args = [q, kb, vb]
    if fox:
        in_specs.append(pl.BlockSpec(ck_t.shape, lambda h, i: (0, 0)))
        args.append(ck_t)
    body = _fox_prompt_body if fox else _sb_prompt_body
    return pl.pallas_call(
        functools.partial(body, tq=tq, tk=tk, g=g, hd=hd),
        grid=(n_kv, seq // tq),
        in_specs=in_specs,
        out_specs=pl.BlockSpec((tq, g * hd), lambda h, i: (i, h)),
        out_shape=jax.ShapeDtypeStruct((seq, n_kv * g * hd), BF16),
        compiler_params=_params("arbitrary", "arbitrary"),
        name="fox_prompt_attention" if fox else "sb_prompt_attention",
    )(*args)


def _cumsum_body(lf_ref, o_ref, carry_ref, *, tk):
    @pl.when(pl.program_id(0) == 0)
    def _():
        carry_ref[...] = jnp.zeros_like(carry_ref)
    r = lax.broadcasted_iota(jnp.int32, (tk, tk), 0)
    c = lax.broadcasted_iota(jnp.int32, (tk, tk), 1)
    upper = (r <= c).astype(F32)
    cum = jnp.dot(lf_ref[...], upper, preferred_element_type=F32,
                  precision=lax.Precision.HIGHEST) + carry_ref[:, 0:1]
    o_ref[...] = cum * LOG2E
    carry_ref[...] = jnp.broadcast_to(cum[:, tk - 1:tk], carry_ref.shape)


def _cumsum_lanes(lf_t):
    nh, seq = lf_t.shape
    tk = CUMSUM_TILE
    return pl.pallas_call(
        functools.partial(_cumsum_body, tk=tk),
        grid=(seq // tk,),
        in_specs=[pl.BlockSpec((nh, tk), lambda j: (0, j))],
        out_specs=pl.BlockSpec((nh, tk), lambda j: (0, j)),
        out_shape=jax.ShapeDtypeStruct((nh, seq), F32),
        scratch_shapes=[pltpu.VMEM((nh, LANES), F32)],
        compiler_params=_params("arbitrary"),
        name="logf_cumsum",
    )(lf_t)


def _sample_att_body(pt_ref, q_ref, kn_ref, vn_ref, *rest, n_pages, page, n_kv, g, hd, t_new, fox):
    if fox:
        lfn_ref, kpool, vpool, lfpool, o_ref, kbuf, vbuf, lfbuf, sem = rest
    else:
        kpool, vpool, o_ref, kbuf, vbuf, sem = rest
    b = pl.program_id(0)
    nb = pl.num_programs(0)
    slot = b % 2
    past = n_pages * page
    rows = n_kv * g * t_new

    def copies(bb, sl):
        out = []
        for p in range(n_pages):
            pid = pt_ref[bb, p]
            out.append(pltpu.make_async_copy(kpool.at[pid], kbuf.at[sl, p], sem.at[0, sl]))
            out.append(pltpu.make_async_copy(vpool.at[pid], vbuf.at[sl, p], sem.at[1, sl]))
            if fox:
                out.append(pltpu.make_async_copy(lfpool.at[pid], lfbuf.at[sl, p], sem.at[2, sl]))
        return out

    @pl.when(b == 0)
    def _():
        for cp in copies(0, 0):
            cp.start()

    @pl.when(b + 1 < nb)
    def _():
        for cp in copies(b + 1, 1 - slot):
            cp.start()

    for cp in copies(b, slot):
        cp.wait()

    q = q_ref[0]
    kn = jnp.concatenate([kn_ref[0], jnp.zeros((LANES - t_new, n_kv * hd), F32)], axis=0)
    vn = jnp.concatenate([vn_ref[0], jnp.zeros((LANES - t_new, n_kv * hd), F32)], axis=0)

    z_parts, zn_parts = [], []
    for h in range(n_kv):
        qh = jnp.concatenate([q[:, (h * g + i) * hd:(h * g + i + 1) * hd] for i in range(g)], axis=0)
        kh = kbuf[slot, :, pl.ds(h, page, stride=n_kv), :].reshape(past, hd).astype(BF16)
        z_parts.append(lax.dot_general(qh, kh, (((1,), (1,)), ((), ())), preferred_element_type=F32))
        knh = kn[:, h * hd:(h + 1) * hd].astype(BF16)
        zn_parts.append(lax.dot_general(qh, knh, (((1,), (1,)), ((), ())), preferred_element_type=F32))
    z = jnp.concatenate(z_parts, axis=0)
    zn = jnp.concatenate(zn_parts, axis=0)

    r_io = lax.broadcasted_iota(jnp.int32, (rows, LANES), 0)
    c_io = lax.broadcasted_iota(jnp.int32, (rows, LANES), 1)
    t_row = r_io % t_new

    if not fox:
        valid_n = c_io < t_row
        cn = _cum_dot(jnp.where(valid_n, _softplus2(zn), 0.0), _tri_incl(LANES))
        wn = jnp.where(valid_n, jnp.exp2(zn - cn), 0.0)
        run = cn[:, 0:1]
        tk = CUMSUM_TILE
        tri = _tri_incl(tk)
        w_parts = [None] * (past // tk)
        for j in reversed(range(past // tk)):
            zb = z[:, j * tk:(j + 1) * tk]
            c = _cum_dot(_softplus2(zb), tri)
            w_parts[j] = jnp.exp2(zb - c - run)
            run = run + c[:, 0:1]
        w = jnp.concatenate(w_parts, axis=1)
        denom = None
    else:
        lfn = lfn_ref[0]
        nh = n_kv * g
        tri = _tri_strict(page)
        bias_n = jnp.dot(lfn, tri, preferred_element_type=F32, precision=lax.Precision.HIGHEST)
        carry = jnp.sum(lfn, axis=-1, keepdims=True)
        bias_parts = [None] * n_pages
        for p in reversed(range(n_pages)):
            lfp = lfbuf[slot, p]
            bias_parts[p] = jnp.dot(lfp, tri, preferred_element_type=F32,
                                    precision=lax.Precision.HIGHEST) + carry
            carry = carry + jnp.sum(lfp, axis=-1, keepdims=True)
        bias_h = jnp.concatenate(bias_parts, axis=1)
        bias = jnp.concatenate([jnp.broadcast_to(bias_h[i:i + 1], (t_new, past)) for i in range(nh)], axis=0)
        bias_new = jnp.concatenate([jnp.broadcast_to(bias_n[i:i + 1], (t_new, LANES)) for i in range(nh)], axis=0)
        s = z + bias * LOG2E
        sn = jnp.where(c_io <= t_row, zn + bias_new * LOG2E, NEG_BIG)
        m = jnp.maximum(jnp.max(s, axis=-1, keepdims=True), jnp.max(sn, axis=-1, keepdims=True))
        w = jnp.exp2(s - m)
        wn = jnp.exp2(sn - m)
        denom = jnp.sum(w, axis=-1, keepdims=True) + jnp.sum(wn, axis=-1, keepdims=True)

    wb = w.astype(BF16)
    wnb = wn.astype(BF16)
    o_cols = []
    rh = g * t_new
    for h in range(n_kv):
        vh = vbuf[slot, :, pl.ds(h, page, stride=n_kv), :].reshape(past, hd).astype(BF16)
        vnh = vn[:, h * hd:(h + 1) * hd].astype(BF16)
        oh = (jnp.dot(wb[h * rh:(h + 1) * rh], vh, preferred_element_type=F32)
              + jnp.dot(wnb[h * rh:(h + 1) * rh], vnh, preferred_element_type=F32))
        if fox:
            oh = oh / denom[h * rh:(h + 1) * rh]
        for i in range(g):
            o_cols.append(oh[i * t_new:(i + 1) * t_new])
    o_ref[0] = jnp.concatenate(o_cols, axis=1).astype(BF16)


def _sample_attention(page_table, q_b, k_new, v_new, k_pool, v_pool, lf_new_t, lf_pool_t, *, n_kv, g, hd):
    nb, t_new, dq = q_b.shape
    n_pages = page_table.shape[1]
    page = k_pool.shape[1] // n_kv
    nkv = n_kv * hd
    fox = lf_pool_t is not None
    in_specs = [pl.BlockSpec((1, t_new, dq), lambda b, pt: (b, 0, 0)),
                pl.BlockSpec((1, t_new, nkv), lambda b, pt: (b, 0, 0)),
                pl.BlockSpec((1, t_new, nkv), lambda b, pt: (b, 0, 0))]
    args = [q_b, k_new, v_new]
    scratch = [pltpu.VMEM((2, n_pages, page * n_kv, hd), F32),
               pltpu.VMEM((2, n_pages, page * n_kv, hd), F32)]
    if fox:
        nh = n_kv * g
        in_specs.append(pl.BlockSpec((1, nh, LANES), lambda b, pt: (b, 0, 0)))
        args.append(lf_new_t)
    in_specs += [pl.BlockSpec(memory_space=pl.ANY), pl.BlockSpec(memory_space=pl.ANY)]
    args += [k_pool, v_pool]
    if fox:
        in_specs.append(pl.BlockSpec(memory_space=pl.ANY))
        args.append(lf_pool_t)
        scratch.append(pltpu.VMEM((2, n_pages, nh, page), F32))
    scratch.append(pltpu.SemaphoreType.DMA((3, 2)))
    return pl.pallas_call(
        functools.partial(_sample_att_body, n_pages=n_pages, page=page, n_kv=n_kv, g=g, hd=hd,
                          t_new=t_new, fox=fox),
        grid_spec=pltpu.PrefetchScalarGridSpec(
            num_scalar_prefetch=1, grid=(nb,), in_specs=in_specs,
            out_specs=pl.BlockSpec((1, t_new, dq), lambda b, pt: (b, 0, 0)),
            scratch_shapes=scratch),
        out_shape=jax.ShapeDtypeStruct((nb, t_new, dq), BF16),
        compiler_params=_params("arbitrary"),
        name="fox_sample_attention" if fox else "sb_sample_attention",
    )(page_table, *args)


def _layernorm(u, gamma, beta):
    mu = jnp.mean(u, axis=-1, keepdims=True)
    uc = u - mu
    var = jnp.mean(uc * uc, axis=-1, keepdims=True)
    return uc * lax.rsqrt(var + LN_EPS) * gamma + beta


def _first_index_of_max(vals, mask, lane):
    top = jnp.max(jnp.where(mask, vals, NEG_BIG), axis=-1, keepdims=True)
    idx = jnp.min(jnp.where(mask & (vals == top), lane, 2 * LANES), axis=-1, keepdims=True)
    return top, idx


def _post_attn_body(op_ref, os_ref, xp_ref, xs_ref, g1_ref, sh2_ref, sc2_ref, w_ref, lng_ref, lnb_ref, rw_ref,
                    rb_ref, x1_ref, h2_ref, eid_ref, gate_ref, *, alpha, n_exp, n_prompt_tiles):
    is_prompt = pl.program_id(0) < n_prompt_tiles
    o = jnp.where(is_prompt, op_ref[...], os_ref[...])
    x = jnp.where(is_prompt, xp_ref[...], xs_ref[...])
    y = jnp.dot(o, w_ref[...], preferred_element_type=F32)
    x1 = _layernorm(alpha * x + g1_ref[0] * y, lng_ref[...], lnb_ref[...])
    x1_ref[...] = x1
    h2 = x1 * (1.0 + sc2_ref[0]) + sh2_ref[0]
    h2_ref[...] = h2

    logits = _dot_split(h2, rw_ref[...])
    lane = lax.broadcasted_iota(jnp.int32, logits.shape, 1)
    real = lane < n_exp
    logits = jnp.where(real, logits, NEG_BIG)
    e = jnp.exp(logits - jnp.max(logits, axis=-1, keepdims=True))
    probs = e / jnp.sum(e, axis=-1, keepdims=True)
    sel = probs + rb_ref[...]
    epg = n_exp // N_GROUPS
    best = first = second = None
    for grp in range(N_GROUPS):
        in_g = (lane >= grp * epg) & (lane < (grp + 1) * epg)
        top1, idx1 = _first_index_of_max(sel, in_g, lane)
        top2, idx2 = _first_index_of_max(sel, in_g & (lane != idx1), lane)
        score = top1 + top2
        if grp == 0:
            best, first, second = score, idx1, idx2
        else:
            better = score > best
            best = jnp.where(better, score, best)
            first = jnp.where(better, idx1, first)
            second = jnp.where(better, idx2, second)
    gate1 = jnp.sum(jnp.where(lane == first, probs, 0.0), axis=-1, keepdims=True)
    gate2 = jnp.sum(jnp.where(lane == second, probs, 0.0), axis=-1, keepdims=True)
    tot = gate1 + gate2
    eid_ref[...] = jnp.where(lane == 0, first, jnp.where(lane == 1, second, 0))
    gate_ref[...] = jnp.where(lane == 0, gate1 / tot, jnp.where(lane == 1, gate2 / tot, 0.0))


def _post_attention(o_p, o_s, x_p, x_s, mod, layer, w_out, ln_g, ln_b, rw_pad, rb_pad, *, alpha, n_exp):
    n_p, d = x_p.shape
    n = n_p + x_s.shape[0]
    tm = TOKEN_TILE
    n_prompt_tiles = n_p // tm
    kind = lambda i: i // n_prompt_tiles
    row = pl.BlockSpec((tm, d), lambda i: (i, 0))
    row_p = _prompt_rows(tm, d, n_prompt_tiles)
    row_s = _sample_rows(tm, d, n_prompt_tiles)
    modspec = lambda c: pl.BlockSpec((1, tm, d), lambda i: (layer, kind(i), c))
    const = lambda a: pl.BlockSpec(a.shape, lambda i: (0,) * a.ndim)
    lanes = pl.BlockSpec((tm, LANES), lambda i: (i, 0))
    return pl.pallas_call(
        functools.partial(_post_attn_body, alpha=alpha, n_exp=n_exp, n_prompt_tiles=n_prompt_tiles),
        grid=(n // tm,),
        in_specs=[row_p, row_s, row_p, row_s, modspec(2), modspec(3), modspec(4), const(w_out), const(ln_g),
                  const(ln_b), const(rw_pad), const(rb_pad)],
        out_specs=[row, row, lanes, lanes],
        out_shape=[jax.ShapeDtypeStruct((n, d), F32), jax.ShapeDtypeStruct((n, d), F32),
                   jax.ShapeDtypeStruct((n, LANES), jnp.int32), jax.ShapeDtypeStruct((n, LANES), F32)],
        compiler_params=_params("arbitrary"),
        name="post_attention",
    )(o_p, o_s, x_p, x_s, mod, mod, mod, w_out, ln_g, ln_b, rw_pad, rb_pad)


def _moe_body(texp_ref, trows_ref, src_ref, dst_ref, srcn_ref, gate_ref, wg_ref, wu_ref, wd_ref,
              h_hbm, out_hbm, xbuf, ybuf, sem, *, tm):
    t = pl.program_id(0)
    nt = pl.num_programs(0)
    slot = t % 2
    del texp_ref
    rows_of = lambda i: trows_ref[jnp.clip(i, 0, nt - 1)]

    def start_gather(idx_ref, sl):
        def chunk(c, _):
            for u in range(ROW_DMA_UNROLL):
                r = c * ROW_DMA_UNROLL + u
                pltpu.make_async_copy(h_hbm.at[idx_ref[0, 0, r]], xbuf.at[sl, r], sem.at[0, sl]).start()
            return 0
        lax.fori_loop(0, tm // ROW_DMA_UNROLL, chunk, 0)

    def wait_gather(sl):
        pltpu.make_async_copy(h_hbm.at[pl.ds(0, tm)], xbuf.at[sl], sem.at[0, sl]).wait()

    def start_scatter(sl, n_rows):
        def one(r):
            pltpu.make_async_copy(ybuf.at[sl, r], out_hbm.at[dst_ref[0, 0, r]], sem.at[1, sl]).start()

        def chunk(c, _):
            for u in range(ROW_DMA_UNROLL):
                one(c * ROW_DMA_UNROLL + u)
            return 0
        n_full = n_rows // ROW_DMA_UNROLL
        lax.fori_loop(0, n_full, chunk, 0)
        lax.fori_loop(n_full * ROW_DMA_UNROLL, n_rows, lambda r, _: (one(r), 0)[1], 0)

    def wait_scatter(sl, n_rows):
        p = tm
        while p >= 1:
            @pl.when((n_rows & p) != 0)
            def _(p=p):
                pltpu.make_async_copy(ybuf.at[sl, pl.ds(0, p)], out_hbm.at[pl.ds(0, p)], sem.at[1, sl]).wait()
            p //= 2

    @pl.when((t == 0) & (rows_of(0) > 0))
    def _():
        start_gather(src_ref, 0)

    @pl.when((t + 1 < nt) & (rows_of(t + 1) > 0))
    def _():
        start_gather(srcn_ref, 1 - slot)

    @pl.when(t >= 2)
    def _():
        wait_scatter(slot, rows_of(t - 2))

    @pl.when(rows_of(t) > 0)
    def _():
        wait_gather(slot)
        xb = xbuf[slot].astype(BF16)
        gp = jnp.dot(xb, wg_ref[0, 0].astype(BF16), preferred_element_type=F32)
        up = jnp.dot(xb, wu_ref[0, 0].astype(BF16), preferred_element_type=F32)
        act = gp / (1.0 + jnp.exp(-gp)) * up * gate_ref[...]
        ybuf[slot] = jnp.dot(act.astype(BF16), wd_ref[0, 0].astype(BF16), preferred_element_type=F32)
        start_scatter(slot, rows_of(t))

    @pl.when(t == nt - 1)
    def _():
        @pl.when(nt >= 2)
        def _():
            wait_scatter(1 - slot, rows_of(t - 1))
        wait_scatter(slot, rows_of(t))


def _moe(h2, tile_expert, tile_rows, row_src, row_dst, row_gate, w_gate, w_up, w_down, layer):
    n, d = h2.shape
    tm = MOE_ROW_TILE
    n_tiles = tile_expert.shape[0]
    f = w_gate.shape[-1]
    nxt = lambda t: jnp.minimum(t + 1, n_tiles - 1)
    smem_rows = lambda imap: pl.BlockSpec((1, 1, tm), imap, memory_space=pltpu.SMEM)
    return pl.pallas_call(
        functools.partial(_moe_body, tm=tm),
        grid_spec=pltpu.PrefetchScalarGridSpec(
            num_scalar_prefetch=2, grid=(n_tiles,),
            in_specs=[smem_rows(lambda t, te, tr: (t, 0, 0)),
                      smem_rows(lambda t, te, tr: (t, 0, 0)),
                      smem_rows(lambda t, te, tr: (nxt(t), 0, 0)),
                      pl.BlockSpec((tm, 1), lambda t, te, tr: (t, 0)),
                      pl.BlockSpec((1, 1, d, f), lambda t, te, tr: (layer, te[t], 0, 0)),
                      pl.BlockSpec((1, 1, d, f), lambda t, te, tr: (layer, te[t], 0, 0)),
                      pl.BlockSpec((1, 1, f, d), lambda t, te, tr: (layer, te[t], 0, 0)),
                      pl.BlockSpec(memory_space=pl.ANY)],
            out_specs=pl.BlockSpec(memory_space=pl.ANY),
            scratch_shapes=[pltpu.VMEM((2, tm, d), F32), pltpu.VMEM((2, tm, d), F32),
                            pltpu.SemaphoreType.DMA((2, 2))]),
        out_shape=jax.ShapeDtypeStruct((TOP_K * n, d), F32),
        compiler_params=_params("arbitrary"),
        name="sparse_experts",
    )(tile_expert, tile_rows, row_src, row_dst, row_src, row_gate, w_gate, w_up, w_down, h2)


def _routing_tables(eid, gate, n_exp, tm):
    n = eid.shape[0]
    n_assign = n * TOP_K
    n_tiles = (n_assign + n_exp * (tm - 1)) // tm + 1
    e_flat = eid.reshape(n_assign)
    order = jnp.argsort(e_flat, stable=True).astype(jnp.int32)
    counts = jnp.sum((e_flat[:, None] == jnp.arange(n_exp, dtype=jnp.int32)[None, :]).astype(jnp.int32), axis=0)
    starts = jnp.cumsum(counts) - counts
    tiles_per = (counts + tm - 1) // tm
    tile_starts = jnp.cumsum(tiles_per) - tiles_per
    n_used = jnp.sum(tiles_per).astype(jnp.int32)
    tile_ids = jnp.arange(n_tiles, dtype=jnp.int32)
    tile_ends = tile_starts + tiles_per
    tile_expert = jnp.sum((tile_ids[:, None] >= tile_ends[None, :]).astype(jnp.int32), axis=1)
    tile_expert = jnp.minimum(tile_expert, n_exp - 1)
    row_in_tile = jnp.arange(tm, dtype=jnp.int32)[None, :]
    within = (tile_ids - tile_starts[tile_expert])[:, None] * tm + row_in_tile
    valid = (within < counts[tile_expert][:, None]) & (tile_ids[:, None] < n_used)
    sorted_pos = jnp.clip(starts[tile_expert][:, None] + within, 0, n_assign - 1)
    assign = order[sorted_pos]
    tok = assign // TOP_K
    slot_k = assign % TOP_K
    row_src = jnp.where(valid, tok, 0).astype(jnp.int32)
    row_dst = jnp.where(valid, slot_k * n + tok, 0).astype(jnp.int32)
    row_gate = jnp.where(valid, gate.reshape(n_assign)[assign], 0.0).astype(F32)
    tile_rows = jnp.sum(valid.astype(jnp.int32), axis=1)
    return (tile_expert, tile_rows, row_src.reshape(n_tiles, 1, tm),
            row_dst.reshape(n_tiles, 1, tm), row_gate.reshape(n_tiles * tm, 1))


def _combine_body(x1_ref, ya_ref, yb_ref, g2_ref, lng_ref, lnb_ref, op_ref, os_ref, *, alpha, n_prompt_tiles,
                  nb, seq_major):
    i = pl.program_id(0)
    y = ya_ref[...] + yb_ref[...]
    out = _layernorm(alpha * x1_ref[...] + g2_ref[0] * y, lng_ref[...], lnb_ref[...])

    @pl.when(i < n_prompt_tiles)
    def _():
        op_ref[...] = out

    @pl.when(i >= n_prompt_tiles)
    def _():
        if seq_major:
            d = out.shape[1]
            for t in range(out.shape[0] // nb):
                os_ref[:, t * d:(t + 1) * d] = out[t * nb:(t + 1) * nb]
        else:
            os_ref[...] = out


def _combine(x1, y2, mod, layer, ln_g, ln_b, *, n_prompt_tiles, nb, alpha, seq_major):
    n, d = x1.shape
    tm = TOKEN_TILE
    n_p = n_prompt_tiles * tm
    n_s = n - n_p
    kind = lambda i: i // n_prompt_tiles
    row = pl.BlockSpec((tm, d), lambda i: (i, 0))
    const = lambda a: pl.BlockSpec(a.shape, lambda i: (0,) * a.ndim)
    if seq_major:
        s_spec = pl.BlockSpec((nb, (tm // nb) * d), lambda i: (0, jnp.maximum(i - n_prompt_tiles, 0)))
        s_shape = jax.ShapeDtypeStruct((nb, (n_s // nb) * d), F32)
    else:
        s_spec = _sample_rows(tm, d, n_prompt_tiles)
        s_shape = jax.ShapeDtypeStruct((n_s, d), F32)
    return pl.pallas_call(
        functools.partial(_combine_body, alpha=alpha, n_prompt_tiles=n_prompt_tiles, nb=nb, seq_major=seq_major),
        grid=(n // tm,),
        in_specs=[row, row, pl.BlockSpec((tm, d), lambda i: (n // tm + i, 0)),
                  pl.BlockSpec((1, tm, d), lambda i: (layer, kind(i), 5)), const(ln_g), const(ln_b)],
        out_specs=[_prompt_rows(tm, d, n_prompt_tiles), s_spec],
        out_shape=[jax.ShapeDtypeStruct((n_p, d), F32), s_shape],
        compiler_params=_params("arbitrary"),
        name="combine_layernorm",
    )(x1, y2, y2, mod, ln_g, ln_b)


def kernel(x_prompt, x_sample, cache_k_sb, cache_v_sb, cache_k_fox, cache_v_fox, cache_logf_fox, page_table,
           c_prompt, c_sample, ada_w, ada_b, ln_g, ln_b, w_in_sb, w_out_sb, w_in_fox, b_f_fox, w_out_fox,
           router_w, router_b, w_gate, w_up, w_down):
    batch, seq, d = x_prompt.shape
    nb, t_new, _ = x_sample.shape
    depth = ada_w.shape[0]
    n_pool, page, n_kv, hd = cache_k_sb.shape[1:]
    n_heads = d // hd
    g = n_heads // n_kv
    nq, nkv = n_heads * hd, n_kv * hd
    n_exp = router_w.shape[1]
    scale = float(hd) ** -0.5 * LOG2E
    alpha = (2.0 * depth) ** 0.25
    tm = TOKEN_TILE
    n_p, n_s = batch * seq, nb * t_new
    assert batch == 1 and n_p % tm == 0 and n_s % tm == 0 and tm % nb == 0
    assert seq % ATT_KV_TILE == 0 and seq % FOX_KV_TILE == 0 and seq % ATT_Q_TILE == 0
    assert ATT_KV_TILE % ATT_Q_TILE == 0 and FOX_KV_TILE % ATT_Q_TILE == 0 and ATT_KV_TILE % CUMSUM_TILE == 0
    assert (page * page_table.shape[1]) % CUMSUM_TILE == 0
    assert n_exp % N_GROUPS == 0 and n_exp <= LANES and n_heads <= LANES and t_new <= LANES
    n_prompt_tiles = n_p // tm

    x_p = x_prompt.reshape(n_p, d)
    x_s = jnp.transpose(x_sample, (1, 0, 2)).reshape(n_s, d)
    c_rows = jnp.concatenate([jnp.broadcast_to(c_prompt, (tm, d))] + [c_sample] * (tm // nb), axis=0)
    mod = _modulation(c_rows, ada_w, ada_b)

    rw_pad = _hi_lo_columns(router_w)
    rb_pad = jnp.pad(router_b.reshape(1, n_exp), ((0, 0), (0, LANES - n_exp)))

    def to_pos_major(a):
        return jnp.transpose(a, (1, 0, 2)).reshape(n_s, a.shape[-1])

    sb_out, fox_out = [], []
    for layer in range(depth):
        j = layer // 2
        fox = layer % 2 == 1
        if fox:
            w_in, w_out = w_in_fox[j], w_out_fox[j]
            w_main = w_in[:, :nq + 2 * nkv].astype(BF16)
            w_f = _hi_lo_columns(w_in[:, nq + 2 * nkv:])
            b_f = jnp.pad(b_f_fox[j].reshape(1, n_heads), ((0, 0), (0, LANES - n_heads)))
            k_pool, v_pool = cache_k_fox[j], cache_v_fox[j]
        else:
            w_in, w_out = w_in_sb[j], w_out_sb[j]
            w_main, w_f, b_f = w_in.astype(BF16), None, None
            k_pool, v_pool = cache_k_sb[j], cache_v_sb[j]
        res = _qkv_proj(x_p, x_s, mod, layer, w_main, w_f, b_f, nb=nb, nq=nq, nkv=nkv, hd=hd, scale=scale)
        q_p, k_p, v_p, kb_p, vb_p = res[:5]
        q_s = res[5].reshape(nb, t_new, nq)
        k_s, v_s = res[6].reshape(nb, t_new, nkv), res[7].reshape(nb, t_new, nkv)
        k_pool = k_pool.reshape(n_pool, page * n_kv, hd)
        v_pool = v_pool.reshape(n_pool, page * n_kv, hd)
        kv_out = (k_p.reshape(batch, seq, n_kv, hd), v_p.reshape(batch, seq, n_kv, hd),
                  k_s.reshape(nb, t_new, n_kv, hd), v_s.reshape(nb, t_new, n_kv, hd))
        if fox:
            lf_p = res[8][:, :n_heads]
            lf_s = res[9].reshape(nb, t_new, LANES)[:, :, :n_heads]
            ck_t = _cumsum_lanes(jnp.transpose(lf_p))
            lf_new_t = jnp.pad(jnp.transpose(lf_s, (0, 2, 1)), ((0, 0), (0, 0), (0, LANES - t_new)))
            lf_pool_t = jnp.transpose(cache_logf_fox[j], (0, 2, 1))
            o_p = _prompt_attention(q_p, kb_p, vb_p, ck_t, seq=seq, n_kv=n_kv, g=g, hd=hd)
            o_s = _sample_attention(page_table, q_s, k_s, v_s, k_pool, v_pool, lf_new_t, lf_pool_t,
                                    n_kv=n_kv, g=g, hd=hd)
            fox_out.append(kv_out[:2] + (lf_p.reshape(batch, seq, n_heads),) + kv_out[2:] + (lf_s,))
        else:
            o_p = _prompt_attention(q_p, kb_p, vb_p, None, seq=seq, n_kv=n_kv, g=g, hd=hd)
            o_s = _sample_attention(page_table, q_s, k_s, v_s, k_pool, v_pool, None, None,
                                    n_kv=n_kv, g=g, hd=hd)
            sb_out.append(kv_out)
        x1, h2, eid, gate = _post_attention(
            o_p, to_pos_major(o_s), x_p, x_s, mod, layer, w_out.astype(BF16),
            ln_g[layer, 0].reshape(1, d), ln_b[layer, 0].reshape(1, d), rw_pad, rb_pad, alpha=alpha, n_exp=n_exp)
        tables = _routing_tables(eid[:, :TOP_K], gate[:, :TOP_K], n_exp, MOE_ROW_TILE)
        y2 = _moe(h2, *tables, w_gate, w_up, w_down, layer)
        x_p, x_s = _combine(x1, y2, mod, layer, ln_g[layer, 1].reshape(1, d), ln_b[layer, 1].reshape(1, d),
                            n_prompt_tiles=n_prompt_tiles, nb=nb, alpha=alpha, seq_major=layer == depth - 1)

    y_prompt = x_p.reshape(batch, seq, d)
    y_sample = x_s.reshape(nb, t_new, d)
    stack = lambda items, i: jnp.stack([e[i] for e in items])
    return (y_prompt, y_sample,
            stack(sb_out, 0), stack(sb_out, 1), stack(sb_out, 2), stack(sb_out, 3),
            stack(fox_out, 0), stack(fox_out, 1), stack(fox_out, 2),
            stack(fox_out, 3), stack(fox_out, 4), stack(fox_out, 5))
```

```python
import functools

import jax
import jax.numpy as jnp
from jax import lax
from jax.experimental import pallas as pl
from jax.experimental.pallas import tpu as pltpu

F32 = jnp.float32
BF16 = jnp.bfloat16

N_GROUPS = 4
TOP_K = 2
LN_EPS = 1e-5
NEG_BIG = -1e30

TOKEN_TILE = 256
MOE_ROW_TILE = 256
ROW_DMA_UNROLL = 8
ATT_Q_TILE = 256
CUMSUM_TILE = 256
ATT_KV_TILE = 512
FOX_KV_TILE = 1024
LANES = 128
V7X_VMEM_LIMIT = 56 * 1024 * 1024


def _params(*sem):
    return pltpu.CompilerParams(dimension_semantics=sem, vmem_limit_bytes=V7X_VMEM_LIMIT)


LOG2E = 1.4426950408889634


def _log_sigmoid_neg(z):
    return -(jnp.maximum(z, 0.0) + jnp.log(1.0 + jnp.exp(-jnp.abs(z))))


def _softplus2(z2):
    neg_abs = pltpu.bitcast(pltpu.bitcast(z2, jnp.uint32) | jnp.uint32(0x80000000), F32)
    return jnp.maximum(z2, 0.0) + jnp.log2(1.0 + jnp.exp2(neg_abs))


def _cum_dot(a, tri):
    return jnp.dot(a.astype(BF16), tri, preferred_element_type=F32)


def _hi_lo_columns(w):
    w = jnp.pad(w, ((0, 0), (0, LANES - w.shape[1])))
    hi = w.astype(BF16)
    lo = (w - hi.astype(F32)).astype(BF16)
    return jnp.concatenate([hi, lo], axis=1)


def _dot_split(a, w_hl):
    hi = a.astype(BF16)
    lo = (a - hi.astype(F32)).astype(BF16)
    p = jnp.dot(hi, w_hl, preferred_element_type=F32)
    q = jnp.dot(lo, w_hl[:, :LANES], preferred_element_type=F32)
    return p[:, :LANES] + p[:, LANES:] + q


def _tri_incl(n):
    r = lax.broadcasted_iota(jnp.int32, (n, n), 0)
    c = lax.broadcasted_iota(jnp.int32, (n, n), 1)
    return (r >= c).astype(BF16)


def _tri_strict(n):
    r = lax.broadcasted_iota(jnp.int32, (n, n), 0)
    c = lax.broadcasted_iota(jnp.int32, (n, n), 1)
    return (r > c).astype(F32)


def _mod_body(c_ref, w_ref, b_ref, o_ref):
    c = c_ref[...]
    a = c / (1.0 + jnp.exp(-c))
    o_ref[0] = jnp.dot(a.astype(BF16), w_ref[0].astype(BF16),
                       preferred_element_type=F32) + b_ref[0]


def _modulation(c_rows, ada_w, ada_b):
    depth, d, d6 = ada_w.shape
    rows = c_rows.shape[0]
    tn = min(1024, d6)
    return pl.pallas_call(
        _mod_body,
        grid=(depth, d6 // tn),
        in_specs=[pl.BlockSpec((rows, d), lambda l, j: (0, 0)),
                  pl.BlockSpec((1, d, tn), lambda l, j: (l, 0, j)),
                  pl.BlockSpec((1, 1, tn), lambda l, j: (l, 0, j))],
        out_specs=pl.BlockSpec((1, rows, tn), lambda l, j: (l, 0, j)),
        out_shape=jax.ShapeDtypeStruct((depth, rows, d6), F32),
        compiler_params=_params("arbitrary", "arbitrary"),
        name="modulation",
    )(c_rows, ada_w, ada_b.reshape(depth, 1, d6))


def _prompt_rows(tm, width, n_prompt_tiles):
    return pl.BlockSpec((tm, width), lambda i: (jnp.minimum(i, n_prompt_tiles - 1), 0))


def _sample_rows(tm, width, n_prompt_tiles):
    return pl.BlockSpec((tm, width), lambda i: (jnp.maximum(i - n_prompt_tiles, 0), 0))


def _qkv_body(*refs, nq, nkv, scale, fox, n_prompt_tiles, nb):
    if fox:
        (xp_ref, xs_ref, sh_ref, sc_ref, w_ref, wf_ref, bf_ref, qp_ref, kp_ref, vp_ref, kbp_ref, vbp_ref,
         qs_ref, ks_ref, vs_ref, lfp_ref, lfs_ref) = refs
    else:
        (xp_ref, xs_ref, sh_ref, sc_ref, w_ref, qp_ref, kp_ref, vp_ref, kbp_ref, vbp_ref,
         qs_ref, ks_ref, vs_ref) = refs
    i = pl.program_id(0)
    x = jnp.where(i < n_prompt_tiles, xp_ref[...], xs_ref[...])
    h = x * (1.0 + sc_ref[0]) + sh_ref[0]
    p = jnp.dot(h.astype(BF16), w_ref[...], preferred_element_type=F32)
    q = (p[:, :nq] * scale).astype(BF16)
    k = p[:, nq:nq + nkv]
    v = p[:, nq + nkv:nq + 2 * nkv]
    if fox:
        lf = _log_sigmoid_neg(-(_dot_split(h, wf_ref[...]) + bf_ref[...]))

    @pl.when(i < n_prompt_tiles)
    def _():
        qp_ref[...] = q
        n_kv = nkv // kp_ref.shape[1]
        for hh in range(n_kv):
            cols = slice(hh * kp_ref.shape[1], (hh + 1) * kp_ref.shape[1])
            kp_ref[pl.ds(hh, k.shape[0], stride=n_kv), :] = k[:, cols]
            vp_ref[pl.ds(hh, v.shape[0], stride=n_kv), :] = v[:, cols]
        kbp_ref[...] = k.astype(BF16)
        vbp_ref[...] = v.astype(BF16)
        if fox:
            lfp_ref[...] = lf

    @pl.when(i >= n_prompt_tiles)
    def _():
        for t in range(q.shape[0] // nb):
            rows = slice(t * nb, (t + 1) * nb)
            qs_ref[:, t * nq:(t + 1) * nq] = q[rows]
            ks_ref[:, t * nkv:(t + 1) * nkv] = k[rows]
            vs_ref[:, t * nkv:(t + 1) * nkv] = v[rows]
            if fox:
                lfs_ref[:, t * LANES:(t + 1) * LANES] = lf[rows]


def _qkv_proj(x_p, x_s, mod, layer, w_main, w_f, b_f, *, nb, nq, nkv, hd, scale):
    n_p, d = x_p.shape
    n_s = x_s.shape[0]
    tm = TOKEN_TILE
    fox = w_f is not None
    n_prompt_tiles = n_p // tm
    n = n_p + n_s
    t_new = n_s // nb
    per = tm // nb
    kind = lambda i: i // n_prompt_tiles
    in_specs = [_prompt_rows(tm, d, n_prompt_tiles), _sample_rows(tm, d, n_prompt_tiles),
                pl.BlockSpec((1, tm, d), lambda i: (layer, kind(i), 0)),
                pl.BlockSpec((1, tm, d), lambda i: (layer, kind(i), 1)),
                pl.BlockSpec(w_main.shape, lambda i: (0, 0))]
    args = [x_p, x_s, mod, mod, w_main]
    if fox:
        in_specs += [pl.BlockSpec(w_f.shape, lambda i: (0, 0)),
                     pl.BlockSpec(b_f.shape, lambda i: (0, 0))]
        args += [w_f, b_f]
    n_kv = nkv // hd
    p_spec = lambda w: _prompt_rows(tm, w, n_prompt_tiles)
    kv_spec = _prompt_rows(tm * n_kv, hd, n_prompt_tiles)
    s_spec = lambda w: pl.BlockSpec((nb, per * w), lambda i: (0, jnp.maximum(i - n_prompt_tiles, 0)))
    p_shape = lambda w, dt: jax.ShapeDtypeStruct((n_p, w), dt)
    kv_shape = jax.ShapeDtypeStruct((n_p * n_kv, hd), F32)
    s_shape = lambda w, dt: jax.ShapeDtypeStruct((nb, t_new * w), dt)
    out_specs = [p_spec(nq), kv_spec, kv_spec, p_spec(nkv), p_spec(nkv),
                 s_spec(nq), s_spec(nkv), s_spec(nkv)]
    out_shape = [p_shape(nq, BF16), kv_shape, kv_shape, p_shape(nkv, BF16), p_shape(nkv, BF16),
                 s_shape(nq, BF16), s_shape(nkv, F32), s_shape(nkv, F32)]
    if fox:
        out_specs += [p_spec(LANES), s_spec(LANES)]
        out_shape += [p_shape(LANES, F32), s_shape(LANES, F32)]
    return pl.pallas_call(
        functools.partial(_qkv_body, nq=nq, nkv=nkv, scale=scale, fox=fox,
                          n_prompt_tiles=n_prompt_tiles, nb=nb),
        grid=(n // tm,),
        in_specs=in_specs, out_specs=out_specs, out_shape=out_shape,
        compiler_params=_params("arbitrary"),
        name="qkv_fox" if fox else "qkv_sb",
    )(*args)


def _stack_heads(q, g, hd):
    return jnp.concatenate([q[:, i * hd:(i + 1) * hd] for i in range(g)], axis=0)


def _unstack_heads(o, g, tq):
    return jnp.concatenate([o[i * tq:(i + 1) * tq] for i in range(g)], axis=1)


def _causal_sweep(qi, tq, tk, carry, block):
    jd = (qi * tq) // tk
    carry = block(jd * tk, tk, carry, qi * tq - jd * tk)

    def pair(i, c):
        j = jd - 1 - 2 * i
        return block((j - 1) * tk, tk, block(j * tk, tk, c, None), None)
    carry = lax.fori_loop(0, jd // 2, pair, carry)
    return lax.cond(jd % 2 == 1, lambda c: block(0, tk, c, None), lambda c: c, carry)


def _sb_prompt_body(q_ref, k_ref, v_ref, o_ref, *, tq, tk, g, hd):
    qi = pl.program_id(1)
    rows = g * tq
    qs = _stack_heads(q_ref[...], g, hd)
    tc = CUMSUM_TILE
    tri = _tri_incl(tc)

    def block(start, width, carry, diag_off):
        run, acc = carry
        masked = diag_off is not None
        start = start if isinstance(start, int) else pl.multiple_of(start, tq)
        kb = k_ref[pl.ds(start, width), :]
        vb = v_ref[pl.ds(start, width), :]
        z = lax.dot_general(qs, kb, (((1,), (1,)), ((), ())), preferred_element_type=F32)
        sp = _softplus2(z)
        if masked:
            r_io = lax.broadcasted_iota(jnp.int32, (rows, width), 0)
            c_io = lax.broadcasted_iota(jnp.int32, (rows, width), 1)
            valid = c_io < diag_off + (r_io & (tq - 1))
            sp = jnp.where(valid, sp, 0.0)
        w_parts = [None] * (width // tc)
        for i in reversed(range(width // tc)):
            c = _cum_dot(sp[:, i * tc:(i + 1) * tc], tri)
            w_parts[i] = jnp.exp2(z[:, i * tc:(i + 1) * tc] - c - run)
            run = run + c[:, 0:1]
        w = jnp.concatenate(w_parts, axis=1)
        if masked:
            w = jnp.where(valid, w, 0.0)
        acc = acc + jnp.dot(w.astype(BF16), vb, preferred_element_type=F32)
        return run, acc

    carry = (jnp.zeros((rows, 1), F32), jnp.zeros((rows, hd), F32))
    _, acc = _causal_sweep(qi, tq, tk, carry, block)
    o_ref[...] = _unstack_heads(acc, g, tq).astype(BF16)


def _fox_prompt_body(q_ref, k_ref, v_ref, ck_ref, o_ref, *, tq, tk, g, hd):
    h = pl.program_id(0)
    qi = pl.program_id(1)
    rows = g * tq
    qs = _stack_heads(q_ref[...], g, hd)

    def block(start, width, carry, diag_off):
        m, l, acc = carry
        start = start if isinstance(start, int) else pl.multiple_of(start, tq)
        kb = k_ref[pl.ds(start, width), :]
        vb = v_ref[pl.ds(start, width), :]
        z = lax.dot_general(qs, kb, (((1,), (1,)), ((), ())), preferred_element_type=F32)
        s = jnp.concatenate(
            [z[i * tq:(i + 1) * tq] - ck_ref[pl.ds(h * g + i, 1), pl.ds(start, width)] for i in range(g)], axis=0)
        if diag_off is not None:
            r_io = lax.broadcasted_iota(jnp.int32, (rows, width), 0)
            c_io = lax.broadcasted_iota(jnp.int32, (rows, width), 1)
            s = jnp.where(c_io <= diag_off + (r_io & (tq - 1)), s, NEG_BIG)
        m_new = jnp.maximum(m, jnp.max(s, axis=-1, keepdims=True))
        alpha = jnp.exp2(m - m_new)
        p = jnp.exp2(s - m_new)
        l = alpha * l + jnp.sum(p, axis=-1, keepdims=True)
        acc = alpha * acc + jnp.dot(p.astype(BF16), vb, preferred_element_type=F32)
        return m_new, l, acc

    carry = (jnp.full((rows, 1), NEG_BIG, F32), jnp.zeros((rows, 1), F32), jnp.zeros((rows, hd), F32))
    _, l, acc = _causal_sweep(qi, tq, tk, carry, block)
    o_ref[...] = _unstack_heads(acc / l, g, tq).astype(BF16)


def _prompt_attention(q, kb, vb, ck_t, *, seq, n_kv, g, hd):
    fox = ck_t is not None
    tq, tk = ATT_Q_TILE, (FOX_KV_TILE if fox else ATT_KV_TILE)
    in_specs = [pl.BlockSpec((tq, g * hd), lambda h, i: (i, h)),
                pl.BlockSpec((seq, hd), lambda h, i: (0, h)),
                pl.BlockSpec((seq, hd), lambda h, i: (0, h))]
    args = [q, kb, vb]
    if fox:
        in_specs.append(pl.BlockSpec(ck_t.shape, lambda h, i: (0, 0)))
        args.append(ck_t)
    body = _fox_prompt_body if fox else _sb_prompt_body
    return pl.pallas_call(
        functools.partial(body, tq=tq, tk=tk, g=g, hd=hd),
        grid=(n_kv, seq // tq),
        in_specs=in_specs,
        out_specs=pl.BlockSpec((tq, g * hd), lambda h, i: (i, h)),
        out_shape=jax.ShapeDtypeStruct((seq, n_kv * g * hd), BF16),
        compiler_params=_params("arbitrary", "arbitrary"),
        name="fox_prompt_attention" if fox else "sb_prompt_attention",
    )(*args)


def _cumsum_body(lf_ref, o_ref, carry_ref, *, tk):
    @pl.when(pl.program_id(0) == 0)
    def _():
        carry_ref[...] = jnp.zeros_like(carry_ref)
    r = lax.broadcasted_iota(jnp.int32, (tk, tk), 0)
    c = lax.broadcasted_iota(jnp.int32, (tk, tk), 1)
    upper = (r <= c).astype(F32)
    cum = jnp.dot(lf_ref[...], upper, preferred_element_type=F32,
                  precision=lax.Precision.HIGHEST) + carry_ref[:, 0:1]
    o_ref[...] = cum * LOG2E
    carry_ref[...] = jnp.broadcast_to(cum[:, tk - 1:tk], carry_ref.shape)


def _cumsum_lanes(lf_t):
    nh, seq = lf_t.shape
    tk = CUMSUM_TILE
    return pl.pallas_call(
        functools.partial(_cumsum_body, tk=tk),
        grid=(seq // tk,),
        in_specs=[pl.BlockSpec((nh, tk), lambda j: (0, j))],
        out_specs=pl.BlockSpec((nh, tk), lambda j: (0, j)),
        out_shape=jax.ShapeDtypeStruct((nh, seq), F32),
        scratch_shapes=[pltpu.VMEM((nh, LANES), F32)],
        compiler_params=_params("arbitrary"),
        name="logf_cumsum",
    )(lf_t)


def _sample_att_body(pt_ref, q_ref, kn_ref, vn_ref, *rest, n_pages, page, n_kv, g, hd, t_new, fox):
    if fox:
        lfn_ref, kpool, vpool, lfpool, o_ref, kbuf, vbuf, lfbuf, sem = rest
    else:
        kpool, vpool, o_ref, kbuf, vbuf, sem = rest
    b = pl.program_id(0)
    nb = pl.num_programs(0)
    slot = b % 2
    past = n_pages * page
    rows = n_kv * g * t_new

    def copies(bb, sl):
        out = []
        for p in range(n_pages):
            pid = pt_ref[bb, p]
            out.append(pltpu.make_async_copy(kpool.at[pid], kbuf.at[sl, p], sem.at[0, sl]))
            out.append(pltpu.make_async_copy(vpool.at[pid], vbuf.at[sl, p], sem.at[1, sl]))
            if fox:
                out.append(pltpu.make_async_copy(lfpool.at[pid], lfbuf.at[sl, p], sem.at[2, sl]))
        return out

    @pl.when(b == 0)
    def _():
        for cp in copies(0, 0):
            cp.start()

    @pl.when(b + 1 < nb)
    def _():
        for cp in copies(b + 1, 1 - slot):
            cp.start()

    for cp in copies(b, slot):
        cp.wait()

    q = q_ref[0]
    kn = jnp.concatenate([kn_ref[0], jnp.zeros((LANES - t_new, n_kv * hd), F32)], axis=0)
    vn = jnp.concatenate([vn_ref[0], jnp.zeros((LANES - t_new, n_kv * hd), F32)], axis=0)

    z_parts, zn_parts = [], []
    for h in range(n_kv):
        qh = jnp.concatenate([q[:, (h * g + i) * hd:(h * g + i + 1) * hd] for i in range(g)], axis=0)
        kh = kbuf[slot, :, pl.ds(h, page, stride=n_kv), :].reshape(past, hd).astype(BF16)
        z_parts.append(lax.dot_general(qh, kh, (((1,), (1,)), ((), ())), preferred_element_type=F32))
        knh = kn[:, h * hd:(h + 1) * hd].astype(BF16)
        zn_parts.append(lax.dot_general(qh, knh, (((1,), (1,)), ((), ())), preferred_element_type=F32))
    z = jnp.concatenate(z_parts, axis=0)
    zn = jnp.concatenate(zn_parts, axis=0)

    r_io = lax.broadcasted_iota(jnp.int32, (rows, LANES), 0)
    c_io = lax.broadcasted_iota(jnp.int32, (rows, LANES), 1)
    t_row = r_io % t_new

    if not fox:
        valid_n = c_io < t_row
        cn = _cum_dot(jnp.where(valid_n, _softplus2(zn), 0.0), _tri_incl(LANES))
        wn = jnp.where(valid_n, jnp.exp2(zn - cn), 0.0)
        run = cn[:, 0:1]
        tk = CUMSUM_TILE
        tri = _tri_incl(tk)
        w_parts = [None] * (past // tk)
        for j in reversed(range(past // tk)):
            zb = z[:, j * tk:(j + 1) * tk]
            c = _cum_dot(_softplus2(zb), tri)
            w_parts[j] = jnp.exp2(zb - c - run)
            run = run + c[:, 0:1]
        w = jnp.concatenate(w_parts, axis=1)
        denom = None
    else:
        lfn = lfn_ref[0]
        nh = n_kv * g
        tri = _tri_strict(page)
        bias_n = jnp.dot(lfn, tri, preferred_element_type=F32, precision=lax.Precision.HIGHEST)
        carry = jnp.sum(lfn, axis=-1, keepdims=True)
        bias_parts = [None] * n_pages
        for p in reversed(range(n_pages)):
            lfp = lfbuf[slot, p]
            bias_parts[p] = jnp.dot(lfp, tri, preferred_element_type=F32,
                                    precision=lax.Precision.HIGHEST) + carry
            carry = carry + jnp.sum(lfp, axis=-1, keepdims=True)
        bias_h = jnp.concatenate(bias_parts, axis=1)
        bias = jnp.concatenate([jnp.broadcast_to(bias_h[i:i + 1], (t_new, past)) for i in range(nh)], axis=0)
        bias_new = jnp.concatenate([jnp.broadcast_to(bias_n[i:i + 1], (t_new, LANES)) for i in range(nh)], axis=0)
        s = z + bias * LOG2E
        sn = jnp.where(c_io <= t_row, zn + bias_new * LOG2E, NEG_BIG)
        m = jnp.maximum(jnp.max(s, axis=-1, keepdims=True), jnp.max(sn, axis=-1, keepdims=True))
        w = jnp.exp2(s - m)
        wn = jnp.exp2(sn - m)
        denom = jnp.sum(w, axis=-1, keepdims=True) + jnp.sum(wn, axis=-1, keepdims=True)

    wb = w.astype(BF16)
    wnb = wn.astype(BF16)
    o_cols = []
    rh = g * t_new
    for h in range(n_kv):
        vh = vbuf[slot, :, pl.ds(h, page, stride=n_kv), :].reshape(past, hd).astype(BF16)
        vnh = vn[:, h * hd:(h + 1) * hd].astype(BF16)
        oh = (jnp.dot(wb[h * rh:(h + 1) * rh], vh, preferred_element_type=F32)
              + jnp.dot(wnb[h * rh:(h + 1) * rh], vnh, preferred_element_type=F32))
        if fox:
            oh = oh / denom[h * rh:(h + 1) * rh]
        for i in range(g):
            o_cols.append(oh[i * t_new:(i + 1) * t_new])
    o_ref[0] = jnp.concatenate(o_cols, axis=1).astype(BF16)


def _sample_attention(page_table, q_b, k_new, v_new, k_pool, v_pool, lf_new_t, lf_pool_t, *, n_kv, g, hd):
    nb, t_new, dq = q_b.shape
    n_pages = page_table.shape[1]
    page = k_pool.shape[1] // n_kv
    nkv = n_kv * hd
    fox = lf_pool_t is not None
    in_specs = [pl.BlockSpec((1, t_new, dq), lambda b, pt: (b, 0, 0)),
                pl.BlockSpec((1, t_new, nkv), lambda b, pt: (b, 0, 0)),
                pl.BlockSpec((1, t_new, nkv), lambda b, pt: (b, 0, 0))]
    args = [q_b, k_new, v_new]
    scratch = [pltpu.VMEM((2, n_pages, page * n_kv, hd), F32),
               pltpu.VMEM((2, n_pages, page * n_kv, hd), F32)]
    if fox:
        nh = n_kv * g
        in_specs.append(pl.BlockSpec((1, nh, LANES), lambda b, pt: (b, 0, 0)))
        args.append(lf_new_t)
    in_specs += [pl.BlockSpec(memory_space=pl.ANY), pl.BlockSpec(memory_space=pl.ANY)]
    args += [k_pool, v_pool]
    if fox:
        in_specs.append(pl.BlockSpec(memory_space=pl.ANY))
        args.append(lf_pool_t)
        scratch.append(pltpu.VMEM((2, n_pages, nh, page), F32))
    scratch.append(pltpu.SemaphoreType.DMA((3, 2)))
    return pl.pallas_call(
        functools.partial(_sample_att_body, n_pages=n_pages, page=page, n_kv=n_kv, g=g, hd=hd,
                          t_new=t_new, fox=fox),
        grid_spec=pltpu.PrefetchScalarGridSpec(
            num_scalar_prefetch=1, grid=(nb,), in_specs=in_specs,
            out_specs=pl.BlockSpec((1, t_new, dq), lambda b, pt: (b, 0, 0)),
            scratch_shapes=scratch),
        out_shape=jax.ShapeDtypeStruct((nb, t_new, dq), BF16),
        compiler_params=_params("arbitrary"),
        name="fox_sample_attention" if fox else "sb_sample_attention",
    )(page_table, *args)


def _layernorm(u, gamma, beta):
    mu = jnp.mean(u, axis=-1, keepdims=True)
    uc = u - mu
    var = jnp.mean(uc * uc, axis=-1, keepdims=True)
    return uc * lax.rsqrt(var + LN_EPS) * gamma + beta


def _first_index_of_max(vals, mask, lane):
    top = jnp.max(jnp.where(mask, vals, NEG_BIG), axis=-1, keepdims=True)
    idx = jnp.min(jnp.where(mask & (vals == top), lane, 2 * LANES), axis=-1, keepdims=True)
    return top, idx


def _post_attn_body(op_ref, os_ref, xp_ref, xs_ref, g1_ref, sh2_ref, sc2_ref, w_ref, lng_ref, lnb_ref, rw_ref,
                    rb_ref, x1_ref, h2_ref, eid_ref, gate_ref, *, alpha, n_exp, n_prompt_tiles):
    is_prompt = pl.program_id(0) < n_prompt_tiles
    o = jnp.where(is_prompt, op_ref[...], os_ref[...])
    x = jnp.where(is_prompt, xp_ref[...], xs_ref[...])
    y = jnp.dot(o, w_ref[...], preferred_element_type=F32)
    x1 = _layernorm(alpha * x + g1_ref[0] * y, lng_ref[...], lnb_ref[...])
    x1_ref[...] = x1
    h2 = x1 * (1.0 + sc2_ref[0]) + sh2_ref[0]
    h2_ref[...] = h2

    logits = _dot_split(h2, rw_ref[...])
    lane = lax.broadcasted_iota(jnp.int32, logits.shape, 1)
    real = lane < n_exp
    logits = jnp.where(real, logits, NEG_BIG)
    e = jnp.exp(logits - jnp.max(logits, axis=-1, keepdims=True))
    probs = e / jnp.sum(e, axis=-1, keepdims=True)
    sel = probs + rb_ref[...]
    epg = n_exp // N_GROUPS
    best = first = second = None
    for grp in range(N_GROUPS):
        in_g = (lane >= grp * epg) & (lane < (grp + 1) * epg)
        top1, idx1 = _first_index_of_max(sel, in_g, lane)
        top2, idx2 = _first_index_of_max(sel, in_g & (lane != idx1), lane)
        score = top1 + top2
        if grp == 0:
            best, first, second = score, idx1, idx2
        else:
            better = score > best
            best = jnp.where(better, score, best)
            first = jnp.where(better, idx1, first)
            second = jnp.where(better, idx2, second)
    gate1 = jnp.sum(jnp.where(lane == first, probs, 0.0), axis=-1, keepdims=True)
    gate2 = jnp.sum(jnp.where(lane == second, probs, 0.0), axis=-1, keepdims=True)
    tot = gate1 + gate2
    eid_ref[...] = jnp.where(lane == 0, first, jnp.where(lane == 1, second, 0))
    gate_ref[...] = jnp.where(lane == 0, gate1 / tot, jnp.where(lane == 1, gate2 / tot, 0.0))


def _post_attention(o_p, o_s, x_p, x_s, mod, layer, w_out, ln_g, ln_b, rw_pad, rb_pad, *, alpha, n_exp):
    n_p, d = x_p.shape
    n = n_p + x_s.shape[0]
    tm = TOKEN_TILE
    n_prompt_tiles = n_p // tm
    kind = lambda i: i // n_prompt_tiles
    row = pl.BlockSpec((tm, d), lambda i: (i, 0))
    row_p = _prompt_rows(tm, d, n_prompt_tiles)
    row_s = _sample_rows(tm, d, n_prompt_tiles)
    modspec = lambda c: pl.BlockSpec((1, tm, d), lambda i: (layer, kind(i), c))
    const = lambda a: pl.BlockSpec(a.shape, lambda i: (0,) * a.ndim)
    lanes = pl.BlockSpec((tm, LANES), lambda i: (i, 0))
    return pl.pallas_call(
        functools.partial(_post_attn_body, alpha=alpha, n_exp=n_exp, n_prompt_tiles=n_prompt_tiles),
        grid=(n // tm,),
        in_specs=[row_p, row_s, row_p, row_s, modspec(2), modspec(3), modspec(4), const(w_out), const(ln_g),
                  const(ln_b), const(rw_pad), const(rb_pad)],
        out_specs=[row, row, lanes, lanes],
        out_shape=[jax.ShapeDtypeStruct((n, d), F32), jax.ShapeDtypeStruct((n, d), F32),
                   jax.ShapeDtypeStruct((n, LANES), jnp.int32), jax.ShapeDtypeStruct((n, LANES), F32)],
        compiler_params=_params("arbitrary"),
        name="post_attention",
    )(o_p, o_s, x_p, x_s, mod, mod, mod, w_out, ln_g, ln_b, rw_pad, rb_pad)


def _moe_body(texp_ref, trows_ref, src_ref, srcn_ref, gate_ref, wg_ref, wu_ref, wd_ref,
              h_hbm, o_ref, xbuf, sem, *, tm):
    t = pl.program_id(0)
    nt = pl.num_programs(0)
    slot = t % 2
    del texp_ref
    rows_of = lambda i: trows_ref[jnp.clip(i, 0, nt - 1)]

    def start_gather(idx_ref, sl):
        def chunk(c, _):
            for u in range(ROW_DMA_UNROLL):
                r = c * ROW_DMA_UNROLL + u
                pltpu.make_async_copy(h_hbm.at[idx_ref[0, 0, r]], xbuf.at[sl, r], sem.at[sl]).start()
            return 0
        lax.fori_loop(0, tm // ROW_DMA_UNROLL, chunk, 0)

    def wait_gather(sl):
        pltpu.make_async_copy(h_hbm.at[pl.ds(0, tm)], xbuf.at[sl], sem.at[sl]).wait()

    @pl.when((t == 0) & (rows_of(0) > 0))
    def _():
        start_gather(src_ref, 0)

    @pl.when((t + 1 < nt) & (rows_of(t + 1) > 0))
    def _():
        start_gather(srcn_ref, 1 - slot)

    @pl.when(rows_of(t) > 0)
    def _():
        wait_gather(slot)
        xb = xbuf[slot].astype(BF16)
        gp = jnp.dot(xb, wg_ref[0, 0].astype(BF16), preferred_element_type=F32)
        up = jnp.dot(xb, wu_ref[0, 0].astype(BF16), preferred_element_type=F32)
        act = gp / (1.0 + jnp.exp(-gp)) * up * gate_ref[...]
        o_ref[...] = jnp.dot(act.astype(BF16), wd_ref[0, 0].astype(BF16), preferred_element_type=F32)

    @pl.when(rows_of(t) == 0)
    def _():
        o_ref[...] = jnp.zeros_like(o_ref)


def _moe(h2, tile_expert, tile_rows, row_src, row_gate, w_gate, w_up, w_down, layer):
    n, d = h2.shape
    tm = MOE_ROW_TILE
    n_tiles = tile_expert.shape[0]
    f = w_gate.shape[-1]
    nxt = lambda t: jnp.minimum(t + 1, n_tiles - 1)
    smem_rows = lambda imap: pl.BlockSpec((1, 1, tm), imap, memory_space=pltpu.SMEM)
    return pl.pallas_call(
        functools.partial(_moe_body, tm=tm),
        grid_spec=pltpu.PrefetchScalarGridSpec(
            num_scalar_prefetch=2, grid=(n_tiles,),
            in_specs=[smem_rows(lambda t, te, tr: (t, 0, 0)),
                      smem_rows(lambda t, te, tr: (nxt(t), 0, 0)),
                      pl.BlockSpec((tm, 1), lambda t, te, tr: (t, 0)),
                      pl.BlockSpec((1, 1, d, f), lambda t, te, tr: (layer, te[t], 0, 0)),
                      pl.BlockSpec((1, 1, d, f), lambda t, te, tr: (layer, te[t], 0, 0)),
                      pl.BlockSpec((1, 1, f, d), lambda t, te, tr: (layer, te[t], 0, 0)),
                      pl.BlockSpec(memory_space=pl.ANY)],
            out_specs=pl.BlockSpec((tm, d), lambda t, te, tr: (t, 0)),
            scratch_shapes=[pltpu.VMEM((2, tm, d), F32), pltpu.SemaphoreType.DMA((2,))]),
        out_shape=jax.ShapeDtypeStruct((n_tiles * tm, d), F32),
        compiler_params=_params("arbitrary"),
        name="sparse_experts",
    )(tile_expert, tile_rows, row_src, row_src, row_gate, w_gate, w_up, w_down, h2)


def _routing_tables(eid, gate, n_exp, tm, tok_tile):
    n = eid.shape[0]
    n_assign = n * TOP_K
    n_tiles = (n_assign + n_exp * (tm - 1)) // tm + 1
    e_flat = eid.reshape(n_assign)
    order = jnp.argsort(e_flat, stable=True).astype(jnp.int32)
    rank = jnp.argsort(order).astype(jnp.int32)
    counts = jnp.sum((e_flat[:, None] == jnp.arange(n_exp, dtype=jnp.int32)[None, :]).astype(jnp.int32), axis=0)
    starts = jnp.cumsum(counts) - counts
    tiles_per = (counts + tm - 1) // tm
    tile_starts = jnp.cumsum(tiles_per) - tiles_per
    n_used = jnp.sum(tiles_per).astype(jnp.int32)
    tile_ids = jnp.arange(n_tiles, dtype=jnp.int32)
    tile_ends = tile_starts + tiles_per
    tile_expert = jnp.sum((tile_ids[:, None] >= tile_ends[None, :]).astype(jnp.int32), axis=1)
    tile_expert = jnp.minimum(tile_expert, n_exp - 1)
    row_in_tile = jnp.arange(tm, dtype=jnp.int32)[None, :]
    within = (tile_ids - tile_starts[tile_expert])[:, None] * tm + row_in_tile
    valid = (within < counts[tile_expert][:, None]) & (tile_ids[:, None] < n_used)
    sorted_pos = jnp.clip(starts[tile_expert][:, None] + within, 0, n_assign - 1)
    assign = order[sorted_pos]
    row_src = jnp.where(valid, assign // TOP_K, 0).astype(jnp.int32)
    row_gate = jnp.where(valid, gate.reshape(n_assign)[assign], 0.0).astype(F32)
    tile_rows = jnp.sum(valid.astype(jnp.int32), axis=1)
    pos = (tile_starts[e_flat] * tm + rank - starts[e_flat]).astype(jnp.int32).reshape(n, TOP_K)
    pos = jnp.transpose(pos.reshape(n // tok_tile, tok_tile, TOP_K), (0, 2, 1))
    return (tile_expert, tile_rows, row_src.reshape(n_tiles, 1, tm), row_gate.reshape(n_tiles * tm, 1),
            pos.reshape(n // tok_tile, 1, TOP_K * tok_tile))


def _combine_body(pos_ref, posn_ref, x1_ref, g2_ref, lng_ref, lnb_ref, y_hbm, op_ref, os_ref, ybuf, sem, *,
                  alpha, n_prompt_tiles, nb, seq_major):
    i = pl.program_id(0)
    n_steps = pl.num_programs(0)
    slot = i % 2
    tm = x1_ref.shape[0]

    def start_gather(idx_ref, sl):
        def chunk(c, _):
            for u in range(ROW_DMA_UNROLL):
                r = c * ROW_DMA_UNROLL + u
                for k in range(TOP_K):
                    src = y_hbm.at[idx_ref[0, 0, k * tm + r]]
                    pltpu.make_async_copy(src, ybuf.at[sl, k, r], sem.at[sl]).start()
            return 0
        lax.fori_loop(0, tm // ROW_DMA_UNROLL, chunk, 0)

    def wait_gather(sl):
        for k in range(TOP_K):
            pltpu.make_async_copy(y_hbm.at[pl.ds(0, tm)], ybuf.at[sl, k], sem.at[sl]).wait()

    @pl.when(i == 0)
    def _():
        start_gather(pos_ref, 0)

    @pl.when(i + 1 < n_steps)
    def _():
        start_gather(posn_ref, 1 - slot)

    wait_gather(slot)
    y = ybuf[slot, 0]
    for k in range(1, TOP_K):
        y = y + ybuf[slot, k]
    out = _layernorm(alpha * x1_ref[...] + g2_ref[0] * y, lng_ref[...], lnb_ref[...])

    @pl.when(i < n_prompt_tiles)
    def _():
        op_ref[...] = out

    @pl.when(i >= n_prompt_tiles)
    def _():
        if seq_major:
            d = out.shape[1]
            for t in range(out.shape[0] // nb):
                os_ref[:, t * d:(t + 1) * d] = out[t * nb:(t + 1) * nb]
        else:
            os_ref[...] = out


def _combine(x1, y_sorted, pos, mod, layer, ln_g, ln_b, *, n_prompt_tiles, nb, alpha, seq_major):
    n, d = x1.shape
    tm = TOKEN_TILE
    n_steps = n // tm
    n_p = n_prompt_tiles * tm
    n_s = n - n_p
    kind = lambda i: i // n_prompt_tiles
    row = pl.BlockSpec((tm, d), lambda i: (i, 0))
    const = lambda a: pl.BlockSpec(a.shape, lambda i: (0,) * a.ndim)
    smem_pos = lambda imap: pl.BlockSpec((1, 1, TOP_K * tm), imap, memory_space=pltpu.SMEM)
    if seq_major:
        s_spec = pl.BlockSpec((nb, (tm // nb) * d), lambda i: (0, jnp.maximum(i - n_prompt_tiles, 0)))
        s_shape = jax.ShapeDtypeStruct((nb, (n_s // nb) * d), F32)
    else:
        s_spec = _sample_rows(tm, d, n_prompt_tiles)
        s_shape = jax.ShapeDtypeStruct((n_s, d), F32)
    return pl.pallas_call(
        functools.partial(_combine_body, alpha=alpha, n_prompt_tiles=n_prompt_tiles, nb=nb, seq_major=seq_major),
        grid=(n_steps,),
        in_specs=[smem_pos(lambda i: (i, 0, 0)), smem_pos(lambda i: (jnp.minimum(i + 1, n_steps - 1), 0, 0)),
                  row, pl.BlockSpec((1, tm, d), lambda i: (layer, kind(i), 5)), const(ln_g), const(ln_b),
                  pl.BlockSpec(memory_space=pl.ANY)],
        out_specs=[_prompt_rows(tm, d, n_prompt_tiles), s_spec],
        out_shape=[jax.ShapeDtypeStruct((n_p, d), F32), s_shape],
        scratch_shapes=[pltpu.VMEM((2, TOP_K, tm, d), F32), pltpu.SemaphoreType.DMA((2,))],
        compiler_params=_params("arbitrary"),
        name="combine_layernorm",
    )(pos, pos, x1, mod, ln_g, ln_b, y_sorted)


def kernel(x_prompt, x_sample, cache_k_sb, cache_v_sb, cache_k_fox, cache_v_fox, cache_logf_fox, page_table,
           c_prompt, c_sample, ada_w, ada_b, ln_g, ln_b, w_in_sb, w_out_sb, w_in_fox, b_f_fox, w_out_fox,
           router_w, router_b, w_gate, w_up, w_down):
    batch, seq, d = x_prompt.shape
    nb, t_new, _ = x_sample.shape
    depth = ada_w.shape[0]
    n_pool, page, n_kv, hd = cache_k_sb.shape[1:]
    n_heads = d // hd
    g = n_heads // n_kv
    nq, nkv = n_heads * hd, n_kv * hd
    n_exp = router_w.shape[1]
    scale = float(hd) ** -0.5 * LOG2E
    alpha = (2.0 * depth) ** 0.25
    tm = TOKEN_TILE
    n_p, n_s = batch * seq, nb * t_new
    assert batch == 1 and n_p % tm == 0 and n_s % tm == 0 and tm % nb == 0
    assert seq % ATT_KV_TILE == 0 and seq % FOX_KV_TILE == 0 and seq % ATT_Q_TILE == 0
    assert ATT_KV_TILE % ATT_Q_TILE == 0 and FOX_KV_TILE % ATT_Q_TILE == 0 and ATT_KV_TILE % CUMSUM_TILE == 0
    assert (page * page_table.shape[1]) % CUMSUM_TILE == 0
    assert n_exp % N_GROUPS == 0 and n_exp <= LANES and n_heads <= LANES and t_new <= LANES
    n_prompt_tiles = n_p // tm

    x_p = x_prompt.reshape(n_p, d)
    x_s = jnp.transpose(x_sample, (1, 0, 2)).reshape(n_s, d)
    c_rows = jnp.concatenate([jnp.broadcast_to(c_prompt, (tm, d))] + [c_sample] * (tm // nb), axis=0)
    mod = _modulation(c_rows, ada_w, ada_b)

    rw_pad = _hi_lo_columns(router_w)
    rb_pad = jnp.pad(router_b.reshape(1, n_exp), ((0, 0), (0, LANES - n_exp)))

    def to_pos_major(a):
        return jnp.transpose(a, (1, 0, 2)).reshape(n_s, a.shape[-1])

    sb_out, fox_out = [], []
    for layer in range(depth):
        j = layer // 2
        fox = layer % 2 == 1
        if fox:
            w_in, w_out = w_in_fox[j], w_out_fox[j]
            w_main = w_in[:, :nq + 2 * nkv].astype(BF16)
            w_f = _hi_lo_columns(w_in[:, nq + 2 * nkv:])
            b_f = jnp.pad(b_f_fox[j].reshape(1, n_heads), ((0, 0), (0, LANES - n_heads)))
            k_pool, v_pool = cache_k_fox[j], cache_v_fox[j]
        else:
            w_in, w_out = w_in_sb[j], w_out_sb[j]
            w_main, w_f, b_f = w_in.astype(BF16), None, None
            k_pool, v_pool = cache_k_sb[j], cache_v_sb[j]
        res = _qkv_proj(x_p, x_s, mod, layer, w_main, w_f, b_f, nb=nb, nq=nq, nkv=nkv, hd=hd, scale=scale)
        q_p, k_p, v_p, kb_p, vb_p = res[:5]
        q_s = res[5].reshape(nb, t_new, nq)
        k_s, v_s = res[6].reshape(nb, t_new, nkv), res[7].reshape(nb, t_new, nkv)
        k_pool = k_pool.reshape(n_pool, page * n_kv, hd)
        v_pool = v_pool.reshape(n_pool, page * n_kv, hd)
        kv_out = (k_p.reshape(batch, seq, n_kv, hd), v_p.reshape(batch, seq, n_kv, hd),
                  k_s.reshape(nb, t_new, n_kv, hd), v_s.reshape(nb, t_new, n_kv, hd))
        if fox:
            lf_p = res[8][:, :n_heads]
            lf_s = res[9].reshape(nb, t_new, LANES)[:, :, :n_heads]
            ck_t = _cumsum_lanes(jnp.transpose(lf_p))
            lf_new_t = jnp.pad(jnp.transpose(lf_s, (0, 2, 1)), ((0, 0), (0, 0), (0, LANES - t_new)))
            lf_pool_t = jnp.transpose(cache_logf_fox[j], (0, 2, 1))
            o_p = _prompt_attention(q_p, kb_p, vb_p, ck_t, seq=seq, n_kv=n_kv, g=g, hd=hd)
            o_s = _sample_attention(page_table, q_s, k_s, v_s, k_pool, v_pool, lf_new_t, lf_pool_t,
                                    n_kv=n_kv, g=g, hd=hd)
            fox_out.append(kv_out[:2] + (lf_p.reshape(batch, seq, n_heads),) + kv_out[2:] + (lf_s,))
        else:
            o_p = _prompt_attention(q_p, kb_p, vb_p, None, seq=seq, n_kv=n_kv, g=g, hd=hd)
            o_s = _sample_attention(page_table, q_s, k_s, v_s, k_pool, v_pool, None, None,
                                    n_kv=n_kv, g=g, hd=hd)
            sb_out.append(kv_out)
        x1, h2, eid, gate = _post_attention(
            o_p, to_pos_major(o_s), x_p, x_s, mod, layer, w_out.astype(BF16),
            ln_g[layer, 0].reshape(1, d), ln_b[layer, 0].reshape(1, d), rw_pad, rb_pad, alpha=alpha, n_exp=n_exp)
        *tables, pos = _routing_tables(eid[:, :TOP_K], gate[:, :TOP_K], n_exp, MOE_ROW_TILE, tm)
        y_sorted = _moe(h2, *tables, w_gate, w_up, w_down, layer)
        x_p, x_s = _combine(x1, y_sorted, pos, mod, layer, ln_g[layer, 1].reshape(1, d), ln_b[layer, 1].reshape(1, d),
                            n_prompt_tiles=n_prompt_tiles, nb=nb, alpha=alpha, seq_major=layer == depth - 1)

    y_prompt = x_p.reshape(batch, seq, d)
    y_sample = x_s.reshape(nb, t_new, d)
    stack = lambda items, i: jnp.stack([e[i] for e in items])
    return (y_prompt, y_sample,
            stack(sb_out, 0), stack(sb_out, 1), stack(sb_out, 2), stack(sb_out, 3),
            stack(fox_out, 0), stack(fox_out, 1), stack(fox_out, 2),
            stack(fox_out, 3), stack(fox_out, 4), stack(fox_out, 5))
```

```python
import functools

import jax
import jax.numpy as jnp
from jax import lax
from jax.experimental import pallas as pl
from jax.experimental.pallas import tpu as pltpu

F32 = jnp.float32
BF16 = jnp.bfloat16

N_GROUPS = 4
TOP_K = 2
LN_EPS = 1e-5
NEG_BIG = -1e30

TOKEN_TILE = 256
MOE_ROW_TILE = 256
ROW_DMA_UNROLL = 8
ATT_Q_TILE = 256
CUMSUM_TILE = 256
ATT_KV_TILE = 512
FOX_KV_TILE = 1024
LANES = 128
V7X_VMEM_LIMIT = 56 * 1024 * 1024


def _params(*sem):
    return pltpu.CompilerParams(dimension_semantics=sem, vmem_limit_bytes=V7X_VMEM_LIMIT)


LOG2E = 1.4426950408889634


def _log_sigmoid_neg(z):
    return -(jnp.maximum(z, 0.0) + jnp.log(1.0 + jnp.exp(-jnp.abs(z))))


def _softplus2(z2):
    neg_abs = pltpu.bitcast(pltpu.bitcast(z2, jnp.uint32) | jnp.uint32(0x80000000), F32)
    return jnp.maximum(z2, 0.0) + jnp.log2(1.0 + jnp.exp2(neg_abs))


def _cum_dot(a, tri):
    return jnp.dot(a.astype(BF16), tri, preferred_element_type=F32)


def _hi_lo_columns(w):
    w = jnp.pad(w, ((0, 0), (0, LANES - w.shape[1])))
    hi = w.astype(BF16)
    lo = (w - hi.astype(F32)).astype(BF16)
    return jnp.concatenate([hi, lo], axis=1)


def _dot_split(a, w_hl):
    hi = a.astype(BF16)
    lo = (a - hi.astype(F32)).astype(BF16)
    p = jnp.dot(hi, w_hl, preferred_element_type=F32)
    q = jnp.dot(lo, w_hl[:, :LANES], preferred_element_type=F32)
    return p[:, :LANES] + p[:, LANES:] + q


def _tri_incl(n):
    r = lax.broadcasted_iota(jnp.int32, (n, n), 0)
    c = lax.broadcasted_iota(jnp.int32, (n, n), 1)
    return (r >= c).astype(BF16)


def _tri_strict(n):
    r = lax.broadcasted_iota(jnp.int32, (n, n), 0)
    c = lax.broadcasted_iota(jnp.int32, (n, n), 1)
    return (r > c).astype(F32)


def _mod_body(c_ref, w_ref, b_ref, o_ref):
    c = c_ref[...]
    a = c / (1.0 + jnp.exp(-c))
    o_ref[0] = jnp.dot(a.astype(BF16), w_ref[0].astype(BF16),
                       preferred_element_type=F32) + b_ref[0]


def _modulation(c_rows, ada_w, ada_b):
    depth, d, d6 = ada_w.shape
    rows = c_rows.shape[0]
    tn = min(1024, d6)
    return pl.pallas_call(
        _mod_body,
        grid=(depth, d6 // tn),
        in_specs=[pl.BlockSpec((rows, d), lambda l, j: (0, 0)),
                  pl.BlockSpec((1, d, tn), lambda l, j: (l, 0, j)),
                  pl.BlockSpec((1, 1, tn), lambda l, j: (l, 0, j))],
        out_specs=pl.BlockSpec((1, rows, tn), lambda l, j: (l, 0, j)),
        out_shape=jax.ShapeDtypeStruct((depth, rows, d6), F32),
        compiler_params=_params("arbitrary", "arbitrary"),
        name="modulation",
    )(c_rows, ada_w, ada_b.reshape(depth, 1, d6))


def _prompt_rows(tm, width, n_prompt_tiles):
    return pl.BlockSpec((tm, width), lambda i: (jnp.minimum(i, n_prompt_tiles - 1), 0))


def _sample_rows(tm, width, n_prompt_tiles):
    return pl.BlockSpec((tm, width), lambda i: (jnp.maximum(i - n_prompt_tiles, 0), 0))


def _qkv_body(*refs, nq, nkv, scale, fox, n_prompt_tiles, nb):
    if fox:
        (xp_ref, xs_ref, sh_ref, sc_ref, w_ref, wf_ref, bf_ref, qp_ref, kp_ref, vp_ref, kbp_ref, vbp_ref,
         qs_ref, ks_ref, vs_ref, lfp_ref, lfs_ref) = refs
    else:
        (xp_ref, xs_ref, sh_ref, sc_ref, w_ref, qp_ref, kp_ref, vp_ref, kbp_ref, vbp_ref,
         qs_ref, ks_ref, vs_ref) = refs
    i = pl.program_id(0)
    x = jnp.where(i < n_prompt_tiles, xp_ref[...], xs_ref[...])
    h = x * (1.0 + sc_ref[0]) + sh_ref[0]
    p = jnp.dot(h.astype(BF16), w_ref[...], preferred_element_type=F32)
    q = (p[:, :nq] * scale).astype(BF16)
    k = p[:, nq:nq + nkv]
    v = p[:, nq + nkv:nq + 2 * nkv]
    if fox:
        lf = _log_sigmoid_neg(-(_dot_split(h, wf_ref[...]) + bf_ref[...]))

    @pl.when(i < n_prompt_tiles)
    def _():
        qp_ref[...] = q
        n_kv = nkv // kp_ref.shape[1]
        for hh in range(n_kv):
            cols = slice(hh * kp_ref.shape[1], (hh + 1) * kp_ref.shape[1])
            kp_ref[pl.ds(hh, k.shape[0], stride=n_kv), :] = k[:, cols]
            vp_ref[pl.ds(hh, v.shape[0], stride=n_kv), :] = v[:, cols]
        kbp_ref[...] = k.astype(BF16)
        vbp_ref[...] = v.astype(BF16)
        if fox:
            lfp_ref[...] = lf

    @pl.when(i >= n_prompt_tiles)
    def _():
        for t in range(q.shape[0] // nb):
            rows = slice(t * nb, (t + 1) * nb)
            qs_ref[:, t * nq:(t + 1) * nq] = q[rows]
            ks_ref[:, t * nkv:(t + 1) * nkv] = k[rows]
            vs_ref[:, t * nkv:(t + 1) * nkv] = v[rows]
            if fox:
                lfs_ref[:, t * LANES:(t + 1) * LANES] = lf[rows]


def _qkv_proj(x_p, x_s, mod, layer, w_main, w_f, b_f, *, nb, nq, nkv, hd, scale):
    n_p, d = x_p.shape
    n_s = x_s.shape[0]
    tm = TOKEN_TILE
    fox = w_f is not None
    n_prompt_tiles = n_p // tm
    n = n_p + n_s
    t_new = n_s // nb
    per = tm // nb
    kind = lambda i: i // n_prompt_tiles
    in_specs = [_prompt_rows(tm, d, n_prompt_tiles), _sample_rows(tm, d, n_prompt_tiles),
                pl.BlockSpec((1, tm, d), lambda i: (layer, kind(i), 0)),
                pl.BlockSpec((1, tm, d), lambda i: (layer, kind(i), 1)),
                pl.BlockSpec(w_main.shape, lambda i: (0, 0))]
    args = [x_p, x_s, mod, mod, w_main]
    if fox:
        in_specs += [pl.BlockSpec(w_f.shape, lambda i: (0, 0)),
                     pl.BlockSpec(b_f.shape, lambda i: (0, 0))]
        args += [w_f, b_f]
    n_kv = nkv // hd
    p_spec = lambda w: _prompt_rows(tm, w, n_prompt_tiles)
    kv_spec = _prompt_rows(tm * n_kv, hd, n_prompt_tiles)
    s_spec = lambda w: pl.BlockSpec((nb, per * w), lambda i: (0, jnp.maximum(i - n_prompt_tiles, 0)))
    p_shape = lambda w, dt: jax.ShapeDtypeStruct((n_p, w), dt)
    kv_shape = jax.ShapeDtypeStruct((n_p * n_kv, hd), F32)
    s_shape = lambda w, dt: jax.ShapeDtypeStruct((nb, t_new * w), dt)
    out_specs = [p_spec(nq), kv_spec, kv_spec, p_spec(nkv), p_spec(nkv),
                 s_spec(nq), s_spec(nkv), s_spec(nkv)]
    out_shape = [p_shape(nq, BF16), kv_shape, kv_shape, p_shape(nkv, BF16), p_shape(nkv, BF16),
                 s_shape(nq, BF16), s_shape(nkv, F32), s_shape(nkv, F32)]
    if fox:
        out_specs += [p_spec(LANES), s_spec(LANES)]
        out_shape += [p_shape(LANES, F32), s_shape(LANES, F32)]
    return pl.pallas_call(
        functools.partial(_qkv_body, nq=nq, nkv=nkv, scale=scale, fox=fox,
                          n_prompt_tiles=n_prompt_tiles, nb=nb),
        grid=(n // tm,),
        in_specs=in_specs, out_specs=out_specs, out_shape=out_shape,
        compiler_params=_params("arbitrary"),
        name="qkv_fox" if fox else "qkv_sb",
    )(*args)


def _stack_heads(q, g, hd):
    return jnp.concatenate([q[:, i * hd:(i + 1) * hd] for i in range(g)], axis=0)


def _unstack_heads(o, g, tq):
    return jnp.concatenate([o[i * tq:(i + 1) * tq] for i in range(g)], axis=1)


def _causal_sweep(qi, tq, tk, carry, block):
    jd = (qi * tq) // tk
    carry = block(jd * tk, tk, carry, qi * tq - jd * tk)

    def quad(i, c):
        j = jd - 1 - 4 * i
        for u in range(4):
            c = block((j - u) * tk, tk, c, None)
        return c
    carry = lax.fori_loop(0, jd // 4, quad, carry)
    rem = jd % 4

    def two(c):
        return block((rem - 2) * tk, tk, block((rem - 1) * tk, tk, c, None), None)
    carry = lax.cond(rem >= 2, two, lambda c: c, carry)
    return lax.cond(rem % 2 == 1, lambda c: block(0, tk, c, None), lambda c: c, carry)


def _sb_prompt_body(q_ref, k_ref, v_ref, o_ref, *, tq, tk, g, hd):
    qi = pl.program_id(1)
    rows = g * tq
    qs = _stack_heads(q_ref[...], g, hd)
    tc = CUMSUM_TILE
    tri = _tri_incl(tc)

    def block(start, width, carry, diag_off):
        run, acc = carry
        masked = diag_off is not None
        start = start if isinstance(start, int) else pl.multiple_of(start, tq)
        kb = k_ref[pl.ds(start, width), :]
        vb = v_ref[pl.ds(start, width), :]
        z = lax.dot_general(qs, kb, (((1,), (1,)), ((), ())), preferred_element_type=F32)
        sp = _softplus2(z)
        if masked:
            r_io = lax.broadcasted_iota(jnp.int32, (rows, width), 0)
            c_io = lax.broadcasted_iota(jnp.int32, (rows, width), 1)
            valid = c_io < diag_off + (r_io & (tq - 1))
            sp = jnp.where(valid, sp, 0.0)
        w_parts = [None] * (width // tc)
        for i in reversed(range(width // tc)):
            c = _cum_dot(sp[:, i * tc:(i + 1) * tc], tri)
            w_parts[i] = jnp.exp2(z[:, i * tc:(i + 1) * tc] - c - run)
            run = run + c[:, 0:1]
        w = jnp.concatenate(w_parts, axis=1)
        if masked:
            w = jnp.where(valid, w, 0.0)
        acc = acc + jnp.dot(w.astype(BF16), vb, preferred_element_type=F32)
        return run, acc

    carry = (jnp.zeros((rows, 1), F32), jnp.zeros((rows, hd), F32))
    _, acc = _causal_sweep(qi, tq, tk, carry, block)
    o_ref[...] = _unstack_heads(acc, g, tq).astype(BF16)


def _fox_prompt_body(q_ref, k_ref, v_ref, ck_ref, o_ref, *, tq, tk, g, hd):
    h = pl.program_id(0)
    qi = pl.program_id(1)
    rows = g * tq
    qs = _stack_heads(q_ref[...], g, hd)

    def block(start, width, carry, diag_off):
        m, l, acc = carry
        start = start if isinstance(start, int) else pl.multiple_of(start, tq)
        kb = k_ref[pl.ds(start, width), :]
        vb = v_ref[pl.ds(start, width), :]
        z = lax.dot_general(qs, kb, (((1,), (1,)), ((), ())), preferred_element_type=F32)
        s = jnp.concatenate(
            [z[i * tq:(i + 1) * tq] - ck_ref[pl.ds(h * g + i, 1), pl.ds(start, width)] for i in range(g)], axis=0)
        if diag_off is not None:
            r_io = lax.broadcasted_iota(jnp.int32, (rows, width), 0)
            c_io = lax.broadcasted_iota(jnp.int32, (rows, width), 1)
            s = jnp.where(c_io <= diag_off + (r_io & (tq - 1)), s, NEG_BIG)
        m_new = jnp.maximum(m, jnp.max(s, axis=-1, keepdims=True))
        alpha = jnp.exp2(m - m_new)
        p = jnp.exp2(s - m_new)
        l = alpha * l + jnp.sum(p, axis=-1, keepdims=True)
        acc = alpha * acc + jnp.dot(p.astype(BF16), vb, preferred_element_type=F32)
        return m_new, l, acc

    carry = (jnp.full((rows, 1), NEG_BIG, F32), jnp.zeros((rows, 1), F32), jnp.zeros((rows, hd), F32))
    _, l, acc = _causal_sweep(qi, tq, tk, carry, block)
    o_ref[...] = _unstack_heads(acc / l, g, tq).astype(BF16)


def _prompt_attention(q, kb, vb, ck_t, *, seq, n_kv, g, hd):
    fox = ck_t is not None
    tq, tk = ATT_Q_TILE, (FOX_KV_TILE if fox else ATT_KV_TILE)
    in_specs = [pl.BlockSpec((tq, g * hd), lambda h, i: (i, h)),
                pl.BlockSpec((seq, hd), lambda h, i: (0, h)),
                pl.BlockSpec((seq, hd), lambda h, i: (0, h))]
    args = [q, kb, vb]
    if fox:
        in_specs.append(pl.BlockSpec(ck_t.shape, lambda h, i: (0, 0)))
        args.append(ck_t)
    body = _fox_prompt_body if fox else _sb_prompt_body
    return pl.pallas_call(
        functools.partial(body, tq=tq, tk=tk, g=g, hd=hd),
        grid=(n_kv, seq // tq),
        in_specs=in_specs,
        out_specs=pl.BlockSpec((tq, g * hd), lambda h, i: (i, h)),
        out_shape=jax.ShapeDtypeStruct((seq, n_kv * g * hd), BF16),
        compiler_params=_params("arbitrary", "arbitrary"),
        name="fox_prompt_attention" if fox else "sb_prompt_attention",
    )(*args)


def _cumsum_body(lf_ref, o_ref, carry_ref, *, tk):
    @pl.when(pl.program_id(0) == 0)
    def _():
        carry_ref[...] = jnp.zeros_like(carry_ref)
    r = lax.broadcasted_iota(jnp.int32, (tk, tk), 0)
    c = lax.broadcasted_iota(jnp.int32, (tk, tk), 1)
    upper = (r <= c).astype(F32)
    cum = jnp.dot(lf_ref[...], upper, preferred_element_type=F32,
                  precision=lax.Precision.HIGHEST) + carry_ref[:, 0:1]
    o_ref[...] = cum * LOG2E
    carry_ref[...] = jnp.broadcast_to(cum[:, tk - 1:tk], carry_ref.shape)


def _cumsum_lanes(lf_t):
    nh, seq = lf_t.shape
    tk = CUMSUM_TILE
    return pl.pallas_call(
        functools.partial(_cumsum_body, tk=tk),
        grid=(seq // tk,),
        in_specs=[pl.BlockSpec((nh, tk), lambda j: (0, j))],
        out_specs=pl.BlockSpec((nh, tk), lambda j: (0, j)),
        out_shape=jax.ShapeDtypeStruct((nh, seq), F32),
        scratch_shapes=[pltpu.VMEM((nh, LANES), F32)],
        compiler_params=_params("arbitrary"),
        name="logf_cumsum",
    )(lf_t)


def _sample_att_body(pt_ref, q_ref, kn_ref, vn_ref, *rest, n_pages, page, n_kv, g, hd, t_new, fox):
    if fox:
        lfn_ref, kpool, vpool, lfpool, o_ref, kbuf, vbuf, lfbuf, sem = rest
    else:
        kpool, vpool, o_ref, kbuf, vbuf, sem = rest
    b = pl.program_id(0)
    nb = pl.num_programs(0)
    slot = b % 2
    past = n_pages * page
    rows = n_kv * g * t_new

    def copies(bb, sl):
        out = []
        for p in range(n_pages):
            pid = pt_ref[bb, p]
            out.append(pltpu.make_async_copy(kpool.at[pid], kbuf.at[sl, p], sem.at[0, sl]))
            out.append(pltpu.make_async_copy(vpool.at[pid], vbuf.at[sl, p], sem.at[1, sl]))
            if fox:
                out.append(pltpu.make_async_copy(lfpool.at[pid], lfbuf.at[sl, p], sem.at[2, sl]))
        return out

    @pl.when(b == 0)
    def _():
        for cp in copies(0, 0):
            cp.start()

    @pl.when(b + 1 < nb)
    def _():
        for cp in copies(b + 1, 1 - slot):
            cp.start()

    for cp in copies(b, slot):
        cp.wait()

    q = q_ref[0]
    kn = jnp.concatenate([kn_ref[0], jnp.zeros((LANES - t_new, n_kv * hd), F32)], axis=0)
    vn = jnp.concatenate([vn_ref[0], jnp.zeros((LANES - t_new, n_kv * hd), F32)], axis=0)

    z_parts, zn_parts = [], []
    for h in range(n_kv):
        qh = jnp.concatenate([q[:, (h * g + i) * hd:(h * g + i + 1) * hd] for i in range(g)], axis=0)
        kh = kbuf[slot, :, pl.ds(h, page, stride=n_kv), :].reshape(past, hd).astype(BF16)
        z_parts.append(lax.dot_general(qh, kh, (((1,), (1,)), ((), ())), preferred_element_type=F32))
        knh = kn[:, h * hd:(h + 1) * hd].astype(BF16)
        zn_parts.append(lax.dot_general(qh, knh, (((1,), (1,)), ((), ())), preferred_element_type=F32))
    z = jnp.concatenate(z_parts, axis=0)
    zn = jnp.concatenate(zn_parts, axis=0)

    r_io = lax.broadcasted_iota(jnp.int32, (rows, LANES), 0)
    c_io = lax.broadcasted_iota(jnp.int32, (rows, LANES), 1)
    t_row = r_io % t_new

    if not fox:
        valid_n = c_io < t_row
        cn = _cum_dot(jnp.where(valid_n, _softplus2(zn), 0.0), _tri_incl(LANES))
        wn = jnp.where(valid_n, jnp.exp2(zn - cn), 0.0)
        run = cn[:, 0:1]
        tk = CUMSUM_TILE
        tri = _tri_incl(tk)
        w_parts = [None] * (past // tk)
        for j in reversed(range(past // tk)):
            zb = z[:, j * tk:(j + 1) * tk]
            c = _cum_dot(_softplus2(zb), tri)
            w_parts[j] = jnp.exp2(zb - c - run)
            run = run + c[:, 0:1]
        w = jnp.concatenate(w_parts, axis=1)
        denom = None
    else:
        lfn = lfn_ref[0]
        nh = n_kv * g
        tri = _tri_strict(page)
        bias_n = jnp.dot(lfn, tri, preferred_element_type=F32, precision=lax.Precision.HIGHEST)
        carry = jnp.sum(lfn, axis=-1, keepdims=True)
        bias_parts = [None] * n_pages
        for p in reversed(range(n_pages)):
            lfp = lfbuf[slot, p]
            bias_parts[p] = jnp.dot(lfp, tri, preferred_element_type=F32,
                                    precision=lax.Precision.HIGHEST) + carry
            carry = carry + jnp.sum(lfp, axis=-1, keepdims=True)
        bias_h = jnp.concatenate(bias_parts, axis=1)
        bias = jnp.concatenate([jnp.broadcast_to(bias_h[i:i + 1], (t_new, past)) for i in range(nh)], axis=0)
        bias_new = jnp.concatenate([jnp.broadcast_to(bias_n[i:i + 1], (t_new, LANES)) for i in range(nh)], axis=0)
        s = z + bias * LOG2E
        sn = jnp.where(c_io <= t_row, zn + bias_new * LOG2E, NEG_BIG)
        m = jnp.maximum(jnp.max(s, axis=-1, keepdims=True), jnp.max(sn, axis=-1, keepdims=True))
        w = jnp.exp2(s - m)
        wn = jnp.exp2(sn - m)
        denom = jnp.sum(w, axis=-1, keepdims=True) + jnp.sum(wn, axis=-1, keepdims=True)

    wb = w.astype(BF16)
    wnb = wn.astype(BF16)
    o_cols = []
    rh = g * t_new
    for h in range(n_kv):
        vh = vbuf[slot, :, pl.ds(h, page, stride=n_kv), :].reshape(past, hd).astype(BF16)
        vnh = vn[:, h * hd:(h + 1) * hd].astype(BF16)
        oh = (jnp.dot(wb[h * rh:(h + 1) * rh], vh, preferred_element_type=F32)
              + jnp.dot(wnb[h * rh:(h + 1) * rh], vnh, preferred_element_type=F32))
        if fox:
            oh = oh / denom[h * rh:(h + 1) * rh]
        for i in range(g):
            o_cols.append(oh[i * t_new:(i + 1) * t_new])
    o_ref[0] = jnp.concatenate(o_cols, axis=1).astype(BF16)


def _sample_attention(page_table, q_b, k_new, v_new, k_pool, v_pool, lf_new_t, lf_pool_t, *, n_kv, g, hd):
    nb, t_new, dq = q_b.shape
    n_pages = page_table.shape[1]
    page = k_pool.shape[1] // n_kv
    nkv = n_kv * hd
    fox = lf_pool_t is not None
    in_specs = [pl.BlockSpec((1, t_new, dq), lambda b, pt: (b, 0, 0)),
                pl.BlockSpec((1, t_new, nkv), lambda b, pt: (b, 0, 0)),
                pl.BlockSpec((1, t_new, nkv), lambda b, pt: (b, 0, 0))]
    args = [q_b, k_new, v_new]
    scratch = [pltpu.VMEM((2, n_pages, page * n_kv, hd), F32),
               pltpu.VMEM((2, n_pages, page * n_kv, hd), F32)]
    if fox:
        nh = n_kv * g
        in_specs.append(pl.BlockSpec((1, nh, LANES), lambda b, pt: (b, 0, 0)))
        args.append(lf_new_t)
    in_specs += [pl.BlockSpec(memory_space=pl.ANY), pl.BlockSpec(memory_space=pl.ANY)]
    args += [k_pool, v_pool]
    if fox:
        in_specs.append(pl.BlockSpec(memory_space=pl.ANY))
        args.append(lf_pool_t)
        scratch.append(pltpu.VMEM((2, n_pages, nh, page), F32))
    scratch.append(pltpu.SemaphoreType.DMA((3, 2)))
    return pl.pallas_call(
        functools.partial(_sample_att_body, n_pages=n_pages, page=page, n_kv=n_kv, g=g, hd=hd,
                          t_new=t_new, fox=fox),
        grid_spec=pltpu.PrefetchScalarGridSpec(
            num_scalar_prefetch=1, grid=(nb,), in_specs=in_specs,
            out_specs=pl.BlockSpec((1, t_new, dq), lambda b, pt: (b, 0, 0)),
            scratch_shapes=scratch),
        out_shape=jax.ShapeDtypeStruct((nb, t_new, dq), BF16),
        compiler_params=_params("arbitrary"),
        name="fox_sample_attention" if fox else "sb_sample_attention",
    )(page_table, *args)


def _layernorm(u, gamma, beta):
    mu = jnp.mean(u, axis=-1, keepdims=True)
    uc = u - mu
    var = jnp.mean(uc * uc, axis=-1, keepdims=True)
    return uc * lax.rsqrt(var + LN_EPS) * gamma + beta


def _first_index_of_max(vals, mask, lane):
    top = jnp.max(jnp.where(mask, vals, NEG_BIG), axis=-1, keepdims=True)
    idx = jnp.min(jnp.where(mask & (vals == top), lane, 2 * LANES), axis=-1, keepdims=True)
    return top, idx


def _post_attn_body(op_ref, os_ref, xp_ref, xs_ref, g1_ref, sh2_ref, sc2_ref, w_ref, lng_ref, lnb_ref, rw_ref,
                    rb_ref, x1_ref, h2_ref, eid_ref, gate_ref, *, alpha, n_exp, n_prompt_tiles):
    is_prompt = pl.program_id(0) < n_prompt_tiles
    o = jnp.where(is_prompt, op_ref[...], os_ref[...])
    x = jnp.where(is_prompt, xp_ref[...], xs_ref[...])
    y = jnp.dot(o, w_ref[...], preferred_element_type=F32)
    x1 = _layernorm(alpha * x + g1_ref[0] * y, lng_ref[...], lnb_ref[...])
    x1_ref[...] = x1
    h2 = x1 * (1.0 + sc2_ref[0]) + sh2_ref[0]
    h2_ref[...] = h2

    logits = _dot_split(h2, rw_ref[...])
    lane = lax.broadcasted_iota(jnp.int32, logits.shape, 1)
    real = lane < n_exp
    logits = jnp.where(real, logits, NEG_BIG)
    e = jnp.exp(logits - jnp.max(logits, axis=-1, keepdims=True))
    probs = e / jnp.sum(e, axis=-1, keepdims=True)
    sel = probs + rb_ref[...]
    epg = n_exp // N_GROUPS
    best = first = second = None
    for grp in range(N_GROUPS):
        in_g = (lane >= grp * epg) & (lane < (grp + 1) * epg)
        top1, idx1 = _first_index_of_max(sel, in_g, lane)
        top2, idx2 = _first_index_of_max(sel, in_g & (lane != idx1), lane)
        score = top1 + top2
        if grp == 0:
            best, first, second = score, idx1, idx2
        else:
            better = score > best
            best = jnp.where(better, score, best)
            first = jnp.where(better, idx1, first)
            second = jnp.where(better, idx2, second)
    gate1 = jnp.sum(jnp.where(lane == first, probs, 0.0), axis=-1, keepdims=True)
    gate2 = jnp.sum(jnp.where(lane == second, probs, 0.0), axis=-1, keepdims=True)
    tot = gate1 + gate2
    eid_ref[...] = jnp.where(lane == 0, first, jnp.where(lane == 1, second, 0))
    gate_ref[...] = jnp.where(lane == 0, gate1 / tot, jnp.where(lane == 1, gate2 / tot, 0.0))


def _post_attention(o_p, o_s, x_p, x_s, mod, layer, w_out, ln_g, ln_b, rw_pad, rb_pad, *, alpha, n_exp):
    n_p, d = x_p.shape
    n = n_p + x_s.shape[0]
    tm = TOKEN_TILE
    n_prompt_tiles = n_p // tm
    kind = lambda i: i // n_prompt_tiles
    row = pl.BlockSpec((tm, d), lambda i: (i, 0))
    row_p = _prompt_rows(tm, d, n_prompt_tiles)
    row_s = _sample_rows(tm, d, n_prompt_tiles)
    modspec = lambda c: pl.BlockSpec((1, tm, d), lambda i: (layer, kind(i), c))
    const = lambda a: pl.BlockSpec(a.shape, lambda i: (0,) * a.ndim)
    lanes = pl.BlockSpec((tm, LANES), lambda i: (i, 0))
    return pl.pallas_call(
        functools.partial(_post_attn_body, alpha=alpha, n_exp=n_exp, n_prompt_tiles=n_prompt_tiles),
        grid=(n // tm,),
        in_specs=[row_p, row_s, row_p, row_s, modspec(2), modspec(3), modspec(4), const(w_out), const(ln_g),
                  const(ln_b), const(rw_pad), const(rb_pad)],
        out_specs=[row, row, lanes, lanes],
        out_shape=[jax.ShapeDtypeStruct((n, d), F32), jax.ShapeDtypeStruct((n, d), F32),
                   jax.ShapeDtypeStruct((n, LANES), jnp.int32), jax.ShapeDtypeStruct((n, LANES), F32)],
        compiler_params=_params("arbitrary"),
        name="post_attention",
    )(o_p, o_s, x_p, x_s, mod, mod, mod, w_out, ln_g, ln_b, rw_pad, rb_pad)


def _moe_body(texp_ref, trows_ref, src_ref, dst_ref, srcn_ref, gate_ref, wg_ref, wu_ref, wd_ref,
              h_hbm, out_hbm, xbuf, ybuf, sem, *, tm):
    t = pl.program_id(0)
    nt = pl.num_programs(0)
    slot = t % 2
    del texp_ref
    rows_of = lambda i: trows_ref[jnp.clip(i, 0, nt - 1)]

    def start_gather(idx_ref, sl):
        def chunk(c, _):
            for u in range(ROW_DMA_UNROLL):
                r = c * ROW_DMA_UNROLL + u
                pltpu.make_async_copy(h_hbm.at[idx_ref[0, 0, r]], xbuf.at[sl, r], sem.at[0, sl]).start()
            return 0
        lax.fori_loop(0, tm // ROW_DMA_UNROLL, chunk, 0)

    def wait_gather(sl):
        pltpu.make_async_copy(h_hbm.at[pl.ds(0, tm)], xbuf.at[sl], sem.at[0, sl]).wait()

    def start_scatter(sl, n_rows):
        def one(r):
            pltpu.make_async_copy(ybuf.at[sl, r], out_hbm.at[dst_ref[0, 0, r]], sem.at[1, sl]).start()

        def chunk(c, _):
            for u in range(ROW_DMA_UNROLL):
                one(c * ROW_DMA_UNROLL + u)
            return 0
        n_full = n_rows // ROW_DMA_UNROLL
        lax.fori_loop(0, n_full, chunk, 0)
        lax.fori_loop(n_full * ROW_DMA_UNROLL, n_rows, lambda r, _: (one(r), 0)[1], 0)

    def wait_scatter(sl, n_rows):
        p = tm
        while p >= 1:
            @pl.when((n_rows & p) != 0)
            def _(p=p):
                pltpu.make_async_copy(ybuf.at[sl, pl.ds(0, p)], out_hbm.at[pl.ds(0, p)], sem.at[1, sl]).wait()
            p //= 2

    @pl.when((t == 0) & (rows_of(0) > 0))
    def _():
        start_gather(src_ref, 0)

    @pl.when((t + 1 < nt) & (rows_of(t + 1) > 0))
    def _():
        start_gather(srcn_ref, 1 - slot)

    @pl.when(t >= 2)
    def _():
        wait_scatter(slot, rows_of(t - 2))

    @pl.when(rows_of(t) > 0)
    def _():
        wait_gather(slot)
        xb = xbuf[slot].astype(BF16)
        gp = jnp.dot(xb, wg_ref[0, 0].astype(BF16), preferred_element_type=F32)
        up = jnp.dot(xb, wu_ref[0, 0].astype(BF16), preferred_element_type=F32)
        act = gp / (1.0 + jnp.exp(-gp)) * up * gate_ref[...]
        ybuf[slot] = jnp.dot(act.astype(BF16), wd_ref[0, 0].astype(BF16), preferred_element_type=F32)
        start_scatter(slot, rows_of(t))

    @pl.when(t == nt - 1)
    def _():
        @pl.when(nt >= 2)
        def _():
            wait_scatter(1 - slot, rows_of(t - 1))
        wait_scatter(slot, rows_of(t))


def _moe(h2, tile_expert, tile_rows, row_src, row_dst, row_gate, w_gate, w_up, w_down, layer):
    n, d = h2.shape
    tm = MOE_ROW_TILE
    n_tiles = tile_expert.shape[0]
    f = w_gate.shape[-1]
    nxt = lambda t: jnp.minimum(t + 1, n_tiles - 1)
    smem_rows = lambda imap: pl.BlockSpec((1, 1, tm), imap, memory_space=pltpu.SMEM)
    return pl.pallas_call(
        functools.partial(_moe_body, tm=tm),
        grid_spec=pltpu.PrefetchScalarGridSpec(
            num_scalar_prefetch=2, grid=(n_tiles,),
            in_specs=[smem_rows(lambda t, te, tr: (t, 0, 0)),
                      smem_rows(lambda t, te, tr: (t, 0, 0)),
                      smem_rows(lambda t, te, tr: (nxt(t), 0, 0)),
                      pl.BlockSpec((tm, 1), lambda t, te, tr: (t, 0)),
                      pl.BlockSpec((1, 1, d, f), lambda t, te, tr: (layer, te[t], 0, 0)),
                      pl.BlockSpec((1, 1, d, f), lambda t, te, tr: (layer, te[t], 0, 0)),
                      pl.BlockSpec((1, 1, f, d), lambda t, te, tr: (layer, te[t], 0, 0)),
                      pl.BlockSpec(memory_space=pl.ANY)],
            out_specs=pl.BlockSpec(memory_space=pl.ANY),
            scratch_shapes=[pltpu.VMEM((2, tm, d), F32), pltpu.VMEM((2, tm, d), F32),
                            pltpu.SemaphoreType.DMA((2, 2))]),
        out_shape=jax.ShapeDtypeStruct((TOP_K * n, d), F32),
        compiler_params=_params("arbitrary"),
        name="sparse_experts",
    )(tile_expert, tile_rows, row_src, row_dst, row_src, row_gate, w_gate, w_up, w_down, h2)


def _routing_tables(eid, gate, n_exp, tm):
    n = eid.shape[0]
    n_assign = n * TOP_K
    n_tiles = (n_assign + n_exp * (tm - 1)) // tm + 1
    e_flat = eid.reshape(n_assign)
    order = jnp.argsort(e_flat, stable=True).astype(jnp.int32)
    counts = jnp.sum((e_flat[:, None] == jnp.arange(n_exp, dtype=jnp.int32)[None, :]).astype(jnp.int32), axis=0)
    starts = jnp.cumsum(counts) - counts
    tiles_per = (counts + tm - 1) // tm
    tile_starts = jnp.cumsum(tiles_per) - tiles_per
    n_used = jnp.sum(tiles_per).astype(jnp.int32)
    tile_ids = jnp.arange(n_tiles, dtype=jnp.int32)
    tile_ends = tile_starts + tiles_per
    tile_expert = jnp.sum((tile_ids[:, None] >= tile_ends[None, :]).astype(jnp.int32), axis=1)
    tile_expert = jnp.minimum(tile_expert, n_exp - 1)
    row_in_tile = jnp.arange(tm, dtype=jnp.int32)[None, :]
    within = (tile_ids - tile_starts[tile_expert])[:, None] * tm + row_in_tile
    valid = (within < counts[tile_expert][:, None]) & (tile_ids[:, None] < n_used)
    sorted_pos = jnp.clip(starts[tile_expert][:, None] + within, 0, n_assign - 1)
    assign = order[sorted_pos]
    tok = assign // TOP_K
    slot_k = assign % TOP_K
    row_src = jnp.where(valid, tok, 0).astype(jnp.int32)
    row_dst = jnp.where(valid, slot_k * n + tok, 0).astype(jnp.int32)
    row_gate = jnp.where(valid, gate.reshape(n_assign)[assign], 0.0).astype(F32)
    tile_rows = jnp.sum(valid.astype(jnp.int32), axis=1)
    return (tile_expert, tile_rows, row_src.reshape(n_tiles, 1, tm),
            row_dst.reshape(n_tiles, 1, tm), row_gate.reshape(n_tiles * tm, 1))


def _combine_body(x1_ref, ya_ref, yb_ref, g2_ref, lng_ref, lnb_ref, op_ref, os_ref, *, alpha, n_prompt_tiles,
                  nb, seq_major):
    i = pl.program_id(0)
    y = ya_ref[...] + yb_ref[...]
    out = _layernorm(alpha * x1_ref[...] + g2_ref[0] * y, lng_ref[...], lnb_ref[...])

    @pl.when(i < n_prompt_tiles)
    def _():
        op_ref[...] = out

    @pl.when(i >= n_prompt_tiles)
    def _():
        if seq_major:
            d = out.shape[1]
            for t in range(out.shape[0] // nb):
                os_ref[:, t * d:(t + 1) * d] = out[t * nb:(t + 1) * nb]
        else:
            os_ref[...] = out


def _combine(x1, y2, mod, layer, ln_g, ln_b, *, n_prompt_tiles, nb, alpha, seq_major):
    n, d = x1.shape
    tm = TOKEN_TILE
    n_p = n_prompt_tiles * tm
    n_s = n - n_p
    kind = lambda i: i // n_prompt_tiles
    row = pl.BlockSpec((tm, d), lambda i: (i, 0))
    const = lambda a: pl.BlockSpec(a.shape, lambda i: (0,) * a.ndim)
    if seq_major:
        s_spec = pl.BlockSpec((nb, (tm // nb) * d), lambda i: (0, jnp.maximum(i - n_prompt_tiles, 0)))
        s_shape = jax.ShapeDtypeStruct((nb, (n_s // nb) * d), F32)
    else:
        s_spec = _sample_rows(tm, d, n_prompt_tiles)
        s_shape = jax.ShapeDtypeStruct((n_s, d), F32)
    return pl.pallas_call(
        functools.partial(_combine_body, alpha=alpha, n_prompt_tiles=n_prompt_tiles, nb=nb, seq_major=seq_major),
        grid=(n // tm,),
        in_specs=[row, row, pl.BlockSpec((tm, d), lambda i: (n // tm + i, 0)),
                  pl.BlockSpec((1, tm, d), lambda i: (layer, kind(i), 5)), const(ln_g), const(ln_b)],
        out_specs=[_prompt_rows(tm, d, n_prompt_tiles), s_spec],
        out_shape=[jax.ShapeDtypeStruct((n_p, d), F32), s_shape],
        compiler_params=_params("arbitrary"),
        name="combine_layernorm",
    )(x1, y2, y2, mod, ln_g, ln_b)


def kernel(x_prompt, x_sample, cache_k_sb, cache_v_sb, cache_k_fox, cache_v_fox, cache_logf_fox, page_table,
           c_prompt, c_sample, ada_w, ada_b, ln_g, ln_b, w_in_sb, w_out_sb, w_in_fox, b_f_fox, w_out_fox,
           router_w, router_b, w_gate, w_up, w_down):
    batch, seq, d = x_prompt.shape
    nb, t_new, _ = x_sample.shape
    depth = ada_w.shape[0]
    n_pool, page, n_kv, hd = cache_k_sb.shape[1:]
    n_heads = d // hd
    g = n_heads // n_kv
    nq, nkv = n_heads * hd, n_kv * hd
    n_exp = router_w.shape[1]
    scale = float(hd) ** -0.5 * LOG2E
    alpha = (2.0 * depth) ** 0.25
    tm = TOKEN_TILE
    n_p, n_s = batch * seq, nb * t_new
    assert batch == 1 and n_p % tm == 0 and n_s % tm == 0 and tm % nb == 0
    assert seq % ATT_KV_TILE == 0 and seq % FOX_KV_TILE == 0 and seq % ATT_Q_TILE == 0
    assert ATT_KV_TILE % ATT_Q_TILE == 0 and FOX_KV_TILE % ATT_Q_TILE == 0 and ATT_KV_TILE % CUMSUM_TILE == 0
    assert (page * page_table.shape[1]) % CUMSUM_TILE == 0
    assert n_exp % N_GROUPS == 0 and n_exp <= LANES and n_heads <= LANES and t_new <= LANES
    n_prompt_tiles = n_p // tm

    x_p = x_prompt.reshape(n_p, d)
    x_s = jnp.transpose(x_sample, (1, 0, 2)).reshape(n_s, d)
    c_rows = jnp.concatenate([jnp.broadcast_to(c_prompt, (tm, d))] + [c_sample] * (tm // nb), axis=0)
    mod = _modulation(c_rows, ada_w, ada_b)

    rw_pad = _hi_lo_columns(router_w)
    rb_pad = jnp.pad(router_b.reshape(1, n_exp), ((0, 0), (0, LANES - n_exp)))

    def to_pos_major(a):
        return jnp.transpose(a, (1, 0, 2)).reshape(n_s, a.shape[-1])

    sb_out, fox_out = [], []
    for layer in range(depth):
        j = layer // 2
        fox = layer % 2 == 1
        if fox:
            w_in, w_out = w_in_fox[j], w_out_fox[j]
            w_main = w_in[:, :nq + 2 * nkv].astype(BF16)
            w_f = _hi_lo_columns(w_in[:, nq + 2 * nkv:])
            b_f = jnp.pad(b_f_fox[j].reshape(1, n_heads), ((0, 0), (0, LANES - n_heads)))
            k_pool, v_pool = cache_k_fox[j], cache_v_fox[j]
        else:
            w_in, w_out = w_in_sb[j], w_out_sb[j]
            w_main, w_f, b_f = w_in.astype(BF16), None, None
            k_pool, v_pool = cache_k_sb[j], cache_v_sb[j]
        res = _qkv_proj(x_p, x_s, mod, layer, w_main, w_f, b_f, nb=nb, nq=nq, nkv=nkv, hd=hd, scale=scale)
        q_p, k_p, v_p, kb_p, vb_p = res[:5]
        q_s = res[5].reshape(nb, t_new, nq)
        k_s, v_s = res[6].reshape(nb, t_new, nkv), res[7].reshape(nb, t_new, nkv)
        k_pool = k_pool.reshape(n_pool, page * n_kv, hd)
        v_pool = v_pool.reshape(n_pool, page * n_kv, hd)
        kv_out = (k_p.reshape(batch, seq, n_kv, hd), v_p.reshape(batch, seq, n_kv, hd),
                  k_s.reshape(nb, t_new, n_kv, hd), v_s.reshape(nb, t_new, n_kv, hd))
        if fox:
            lf_p = res[8][:, :n_heads]
            lf_s = res[9].reshape(nb, t_new, LANES)[:, :, :n_heads]
            ck_t = _cumsum_lanes(jnp.transpose(lf_p))
            lf_new_t = jnp.pad(jnp.transpose(lf_s, (0, 2, 1)), ((0, 0), (0, 0), (0, LANES - t_new)))
            lf_pool_t = jnp.transpose(cache_logf_fox[j], (0, 2, 1))
            o_p = _prompt_attention(q_p, kb_p, vb_p, ck_t, seq=seq, n_kv=n_kv, g=g, hd=hd)
            o_s = _sample_attention(page_table, q_s, k_s, v_s, k_pool, v_pool, lf_new_t, lf_pool_t,
                                    n_kv=n_kv, g=g, hd=hd)
            fox_out.append(kv_out[:2] + (lf_p.reshape(batch, seq, n_heads),) + kv_out[2:] + (lf_s,))
        else:
            o_p = _prompt_attention(q_p, kb_p, vb_p, None, seq=seq, n_kv=n_kv, g=g, hd=hd)
            o_s = _sample_attention(page_table, q_s, k_s, v_s, k_pool, v_pool, None, None,
                                    n_kv=n_kv, g=g, hd=hd)
            sb_out.append(kv_out)
        x1, h2, eid, gate = _post_attention(
            o_p, to_pos_major(o_s), x_p, x_s, mod, layer, w_out.astype(BF16),
            ln_g[layer, 0].reshape(1, d), ln_b[layer, 0].reshape(1, d), rw_pad, rb_pad, alpha=alpha, n_exp=n_exp)
        tables = _routing_tables(eid[:, :TOP_K], gate[:, :TOP_K], n_exp, MOE_ROW_TILE)
        y2 = _moe(h2, *tables, w_gate, w_up, w_down, layer)
        x_p, x_s = _combine(x1, y2, mod, layer, ln_g[layer, 1].reshape(1, d), ln_b[layer, 1].reshape(1, d),
                            n_prompt_tiles=n_prompt_tiles, nb=nb, alpha=alpha, seq_major=layer == depth - 1)

    y_prompt = x_p.reshape(batch, seq, d)
    y_sample = x_s.reshape(nb, t_new, d)
    stack = lambda items, i: jnp.stack([e[i] for e in items])
    return (y_prompt, y_sample,
            stack(sb_out, 0), stack(sb_out, 1), stack(sb_out, 2), stack(sb_out, 3),
            stack(fox_out, 0), stack(fox_out, 1), stack(fox_out, 2),
            stack(fox_out, 3), stack(fox_out, 4), stack(fox_out, 5))
```

```python
import functools

import jax
import jax.numpy as jnp
from jax import lax
from jax.experimental import pallas as pl
from jax.experimental.pallas import tpu as pltpu

F32 = jnp.float32
BF16 = jnp.bfloat16

N_GROUPS = 4
TOP_K = 2
LN_EPS = 1e-5
NEG_BIG = -1e30

TOKEN_TILE = 256
MOE_ROW_TILE = 256
ROW_DMA_UNROLL = 8
ATT_Q_TILE = 256
CUMSUM_TILE = 256
ATT_KV_TILE = 512
FOX_KV_TILE = 512
LANES = 128
V7X_VMEM_LIMIT = 56 * 1024 * 1024


def _params(*sem):
    return pltpu.CompilerParams(dimension_semantics=sem, vmem_limit_bytes=V7X_VMEM_LIMIT)


LOG2E = 1.4426950408889634


def _log_sigmoid_neg(z):
    return -(jnp.maximum(z, 0.0) + jnp.log(1.0 + jnp.exp(-jnp.abs(z))))


def _softplus2(z2):
    return jnp.maximum(z2, 0.0) + jnp.log2(1.0 + jnp.exp2(-jnp.abs(z2)))


def _cum_dot(a, tri):
    return jnp.dot(a.astype(BF16), tri, preferred_element_type=F32)


def _hi_lo_columns(w):
    w = jnp.pad(w, ((0, 0), (0, LANES - w.shape[1])))
    hi = w.astype(BF16)
    lo = (w - hi.astype(F32)).astype(BF16)
    return jnp.concatenate([hi, lo], axis=1)


def _dot_split(a, w_hl):
    hi = a.astype(BF16)
    lo = (a - hi.astype(F32)).astype(BF16)
    p = jnp.dot(hi, w_hl, preferred_element_type=F32)
    q = jnp.dot(lo, w_hl[:, :LANES], preferred_element_type=F32)
    return p[:, :LANES] + p[:, LANES:] + q


def _tri_incl(n):
    r = lax.broadcasted_iota(jnp.int32, (n, n), 0)
    c = lax.broadcasted_iota(jnp.int32, (n, n), 1)
    return (r >= c).astype(BF16)


def _tri_strict(n):
    r = lax.broadcasted_iota(jnp.int32, (n, n), 0)
    c = lax.broadcasted_iota(jnp.int32, (n, n), 1)
    return (r > c).astype(F32)


def _mod_body(c_ref, w_ref, b_ref, o_ref):
    c = c_ref[...]
    a = c / (1.0 + jnp.exp(-c))
    o_ref[0] = jnp.dot(a.astype(BF16), w_ref[0].astype(BF16),
                       preferred_element_type=F32) + b_ref[0]


def _modulation(c_rows, ada_w, ada_b):
    depth, d, d6 = ada_w.shape
    rows = c_rows.shape[0]
    tn = min(1024, d6)
    return pl.pallas_call(
        _mod_body,
        grid=(depth, d6 // tn),
        in_specs=[pl.BlockSpec((rows, d), lambda l, j: (0, 0)),
                  pl.BlockSpec((1, d, tn), lambda l, j: (l, 0, j)),
                  pl.BlockSpec((1, 1, tn), lambda l, j: (l, 0, j))],
        out_specs=pl.BlockSpec((1, rows, tn), lambda l, j: (l, 0, j)),
        out_shape=jax.ShapeDtypeStruct((depth, rows, d6), F32),
        compiler_params=_params("arbitrary", "arbitrary"),
        name="modulation",
    )(c_rows, ada_w, ada_b.reshape(depth, 1, d6))


def _prompt_rows(tm, width, n_prompt_tiles):
    return pl.BlockSpec((tm, width), lambda i: (jnp.minimum(i, n_prompt_tiles - 1), 0))


def _sample_rows(tm, width, n_prompt_tiles):
    return pl.BlockSpec((tm, width), lambda i: (jnp.maximum(i - n_prompt_tiles, 0), 0))


def _qkv_body(*refs, nq, nkv, scale, fox, n_prompt_tiles, nb):
    if fox:
        (xp_ref, xs_ref, sh_ref, sc_ref, w_ref, wf_ref, bf_ref, qp_ref, kp_ref, vp_ref, kbp_ref, vbp_ref,
         qs_ref, ks_ref, vs_ref, lfp_ref, lfs_ref) = refs
    else:
        (xp_ref, xs_ref, sh_ref, sc_ref, w_ref, qp_ref, kp_ref, vp_ref, kbp_ref, vbp_ref,
         qs_ref, ks_ref, vs_ref) = refs
    i = pl.program_id(0)
    x = jnp.where(i < n_prompt_tiles, xp_ref[...], xs_ref[...])
    h = x * (1.0 + sc_ref[0]) + sh_ref[0]
    p = jnp.dot(h.astype(BF16), w_ref[...], preferred_element_type=F32)
    q = (p[:, :nq] * scale).astype(BF16)
    k = p[:, nq:nq + nkv]
    v = p[:, nq + nkv:nq + 2 * nkv]
    if fox:
        lf = _log_sigmoid_neg(-(_dot_split(h, wf_ref[...]) + bf_ref[...]))

    @pl.when(i < n_prompt_tiles)
    def _():
        qp_ref[...] = q
        n_kv = nkv // kp_ref.shape[1]
        for hh in range(n_kv):
            cols = slice(hh * kp_ref.shape[1], (hh + 1) * kp_ref.shape[1])
            kp_ref[pl.ds(hh, k.shape[0], stride=n_kv), :] = k[:, cols]
            vp_ref[pl.ds(hh, v.shape[0], stride=n_kv), :] = v[:, cols]
        kbp_ref[...] = k.astype(BF16)
        vbp_ref[...] = v.astype(BF16)
        if fox:
            lfp_ref[...] = lf

    @pl.when(i >= n_prompt_tiles)
    def _():
        for t in range(q.shape[0] // nb):
            rows = slice(t * nb, (t + 1) * nb)
            qs_ref[:, t * nq:(t + 1) * nq] = q[rows]
            ks_ref[:, t * nkv:(t + 1) * nkv] = k[rows]
            vs_ref[:, t * nkv:(t + 1) * nkv] = v[rows]
            if fox:
                lfs_ref[:, t * LANES:(t + 1) * LANES] = lf[rows]


def _qkv_proj(x_p, x_s, mod, layer, w_main, w_f, b_f, *, nb, nq, nkv, hd, scale):
    n_p, d = x_p.shape
    n_s = x_s.shape[0]
    tm = TOKEN_TILE
    fox = w_f is not None
    n_prompt_tiles = n_p // tm
    n = n_p + n_s
    t_new = n_s // nb
    per = tm // nb
    kind = lambda i: i // n_prompt_tiles
    in_specs = [_prompt_rows(tm, d, n_prompt_tiles), _sample_rows(tm, d, n_prompt_tiles),
                pl.BlockSpec((1, tm, d), lambda i: (layer, kind(i), 0)),
                pl.BlockSpec((1, tm, d), lambda i: (layer, kind(i), 1)),
                pl.BlockSpec(w_main.shape, lambda i: (0, 0))]
    args = [x_p, x_s, mod, mod, w_main]
    if fox:
        in_specs += [pl.BlockSpec(w_f.shape, lambda i: (0, 0)),
                     pl.BlockSpec(b_f.shape, lambda i: (0, 0))]
        args += [w_f, b_f]
    n_kv = nkv // hd
    p_spec = lambda w: _prompt_rows(tm, w, n_prompt_tiles)
    kv_spec = _prompt_rows(tm * n_kv, hd, n_prompt_tiles)
    s_spec = lambda w: pl.BlockSpec((nb, per * w), lambda i: (0, jnp.maximum(i - n_prompt_tiles, 0)))
    p_shape = lambda w, dt: jax.ShapeDtypeStruct((n_p, w), dt)
    kv_shape = jax.ShapeDtypeStruct((n_p * n_kv, hd), F32)
    s_shape = lambda w, dt: jax.ShapeDtypeStruct((nb, t_new * w), dt)
    out_specs = [p_spec(nq), kv_spec, kv_spec, p_spec(nkv), p_spec(nkv),
                 s_spec(nq), s_spec(nkv), s_spec(nkv)]
    out_shape = [p_shape(nq, BF16), kv_shape, kv_shape, p_shape(nkv, BF16), p_shape(nkv, BF16),
                 s_shape(nq, BF16), s_shape(nkv, F32), s_shape(nkv, F32)]
    if fox:
        out_specs += [p_spec(LANES), s_spec(LANES)]
        out_shape += [p_shape(LANES, F32), s_shape(LANES, F32)]
    return pl.pallas_call(
        functools.partial(_qkv_body, nq=nq, nkv=nkv, scale=scale, fox=fox,
                          n_prompt_tiles=n_prompt_tiles, nb=nb),
        grid=(n // tm,),
        in_specs=in_specs, out_specs=out_specs, out_shape=out_shape,
        compiler_params=_params("arbitrary"),
        name="qkv_fox" if fox else "qkv_sb",
    )(*args)


def _stack_heads(q, g, hd):
    return jnp.concatenate([q[:, i * hd:(i + 1) * hd] for i in range(g)], axis=0)


def _unstack_heads(o, g, tq):
    return jnp.concatenate([o[i * tq:(i + 1) * tq] for i in range(g)], axis=1)


def _causal_sweep(qi, tq, tk, carry, block):
    jd = (qi * tq) // tk
    carry = block(jd * tk, tk, carry, qi * tq - jd * tk)

    def quad(i, c):
        j = jd - 1 - 4 * i
        for u in range(4):
            c = block((j - u) * tk, tk, c, None)
        return c
    carry = lax.fori_loop(0, jd // 4, quad, carry)
    rem = jd % 4

    def two(c):
        return block((rem - 2) * tk, tk, block((rem - 1) * tk, tk, c, None), None)
    carry = lax.cond(rem >= 2, two, lambda c: c, carry)
    return lax.cond(rem % 2 == 1, lambda c: block(0, tk, c, None), lambda c: c, carry)


def _sb_prompt_body(q_ref, k_ref, v_ref, o_ref, *, tq, tk, g, hd):
    qi = pl.program_id(1)
    rows = g * tq
    qs = _stack_heads(q_ref[...], g, hd)
    tc = CUMSUM_TILE
    tri = _tri_incl(tc)

    def block(start, width, carry, diag_off):
        run, acc = carry
        masked = diag_off is not None
        start = start if isinstance(start, int) else pl.multiple_of(start, tq)
        kb = k_ref[pl.ds(start, width), :]
        vb = v_ref[pl.ds(start, width), :]
        z = lax.dot_general(qs, kb, (((1,), (1,)), ((), ())), preferred_element_type=F32)
        sp = _softplus2(z)
        if masked:
            r_io = lax.broadcasted_iota(jnp.int32, (rows, width), 0)
            c_io = lax.broadcasted_iota(jnp.int32, (rows, width), 1)
            valid = c_io < diag_off + (r_io & (tq - 1))
            sp = jnp.where(valid, sp, 0.0)
        w_parts = [None] * (width // tc)
        for i in reversed(range(width // tc)):
            c = _cum_dot(sp[:, i * tc:(i + 1) * tc], tri)
            w_parts[i] = jnp.exp2(z[:, i * tc:(i + 1) * tc] - c - run)
            run = run + c[:, 0:1]
        w = jnp.concatenate(w_parts, axis=1)
        if masked:
            w = jnp.where(valid, w, 0.0)
        acc = acc + jnp.dot(w.astype(BF16), vb, preferred_element_type=F32)
        return run, acc

    carry = (jnp.zeros((rows, 1), F32), jnp.zeros((rows, hd), F32))
    _, acc = _causal_sweep(qi, tq, tk, carry, block)
    o_ref[...] = _unstack_heads(acc, g, tq).astype(BF16)


def _fox_prompt_body(q_ref, k_ref, v_ref, ck_ref, o_ref, *, tq, tk, g, hd):
    h = pl.program_id(0)
    qi = pl.program_id(1)
    rows = g * tq
    qs = _stack_heads(q_ref[...], g, hd)

    def block(start, width, carry, diag_off):
        m, l, acc = carry
        start = start if isinstance(start, int) else pl.multiple_of(start, tq)
        kb = k_ref[pl.ds(start, width), :]
        vb = v_ref[pl.ds(start, width), :]
        z = lax.dot_general(qs, kb, (((1,), (1,)), ((), ())), preferred_element_type=F32)
        s = jnp.concatenate(
            [z[i * tq:(i + 1) * tq] - ck_ref[pl.ds(h * g + i, 1), pl.ds(start, width)] for i in range(g)], axis=0)
        if diag_off is not None:
            r_io = lax.broadcasted_iota(jnp.int32, (rows, width), 0)
            c_io = lax.broadcasted_iota(jnp.int32, (rows, width), 1)
            s = jnp.where(c_io <= diag_off + (r_io & (tq - 1)), s, NEG_BIG)
        m_new = jnp.maximum(m, jnp.max(s, axis=-1, keepdims=True))
        alpha = jnp.exp2(m - m_new)
        p = jnp.exp2(s - m_new)
        l = alpha * l + jnp.sum(p, axis=-1, keepdims=True)
        acc = alpha * acc + jnp.dot(p.astype(BF16), vb, preferred_element_type=F32)
        return m_new, l, acc

    carry = (jnp.full((rows, 1), NEG_BIG, F32), jnp.zeros((rows, 1), F32), jnp.zeros((rows, hd), F32))
    _, l, acc = _causal_sweep(qi, tq, tk, carry, block)
    o_ref[...] = _unstack_heads(acc / l, g, tq).astype(BF16)


def _prompt_attention(q, kb, vb, ck_t, *, seq, n_kv, g, hd):
    fox = ck_t is not None
    tq, tk = ATT_Q_TILE, (FOX_KV_TILE if fox else ATT_KV_TILE)
    in_specs = [pl.BlockSpec((tq, g * hd), lambda h, i: (i, h)),
                pl.BlockSpec((seq, hd), lambda h, i: (0, h)),
                pl.BlockSpec((seq, hd), lambda h, i: (0, h))]
    args = [q, kb, vb]
    if fox:
        in_specs.append(pl.BlockSpec(ck_t.shape, lambda h, i: (0, 0)))
        args.append(ck_t)
    body = _fox_prompt_body if fox else _sb_prompt_body
    return pl.pallas_call(
        functools.partial(body, tq=tq, tk=tk, g=g, hd=hd),
        grid=(n_kv, seq // tq),
        in_specs=in_specs,
        out_specs=pl.BlockSpec((tq, g * hd), lambda h, i: (i, h)),
        out_shape=jax.ShapeDtypeStruct((seq, n_kv * g * hd), BF16),
        compiler_params=_params("arbitrary", "arbitrary"),
        name="fox_prompt_attention" if fox else "sb_prompt_attention",
    )(*args)


def _cumsum_body(lf_ref, o_ref, carry_ref, *, tk):
    @pl.when(pl.program_id(0) == 0)
    def _():
        carry_ref[...] = jnp.zeros_like(carry_ref)
    r = lax.broadcasted_iota(jnp.int32, (tk, tk), 0)
    c = lax.broadcasted_iota(jnp.int32, (tk, tk), 1)
    upper = (r <= c).astype(F32)
    cum = jnp.dot(lf_ref[...], upper, preferred_element_type=F32,
                  precision=lax.Precision.HIGHEST) + carry_ref[:, 0:1]
    o_ref[...] = cum * LOG2E
    carry_ref[...] = jnp.broadcast_to(cum[:, tk - 1:tk], carry_ref.shape)


def _cumsum_lanes(lf_t):
    nh, seq = lf_t.shape
    tk = CUMSUM_TILE
    return pl.pallas_call(
        functools.partial(_cumsum_body, tk=tk),
        grid=(seq // tk,),
        in_specs=[pl.BlockSpec((nh, tk), lambda j: (0, j))],
        out_specs=pl.BlockSpec((nh, tk), lambda j: (0, j)),
        out_shape=jax.ShapeDtypeStruct((nh, seq), F32),
        scratch_shapes=[pltpu.VMEM((nh, LANES), F32)],
        compiler_params=_params("arbitrary"),
        name="logf_cumsum",
    )(lf_t)


def _sample_att_body(pt_ref, q_ref, kn_ref, vn_ref, *rest, n_pages, page, n_kv, g, hd, t_new, fox):
    if fox:
        lfn_ref, kpool, vpool, lfpool, o_ref, kbuf, vbuf, lfbuf, sem = rest
    else:
        kpool, vpool, o_ref, kbuf, vbuf, sem = rest
    b = pl.program_id(0)
    nb = pl.num_programs(0)
    slot = b % 2
    past = n_pages * page
    rows = n_kv * g * t_new

    def copies(bb, sl):
        out = []
        for p in range(n_pages):
            pid = pt_ref[bb, p]
            out.append(pltpu.make_async_copy(kpool.at[pid], kbuf.at[sl, p], sem.at[0, sl]))
            out.append(pltpu.make_async_copy(vpool.at[pid], vbuf.at[sl, p], sem.at[1, sl]))
            if fox:
                out.append(pltpu.make_async_copy(lfpool.at[pid], lfbuf.at[sl, p], sem.at[2, sl]))
        return out

    @pl.when(b == 0)
    def _():
        for cp in copies(0, 0):
            cp.start()

    @pl.when(b + 1 < nb)
    def _():
        for cp in copies(b + 1, 1 - slot):
            cp.start()

    for cp in copies(b, slot):
        cp.wait()

    q = q_ref[0]
    kn = jnp.concatenate([kn_ref[0], jnp.zeros((LANES - t_new, n_kv * hd), F32)], axis=0)
    vn = jnp.concatenate([vn_ref[0], jnp.zeros((LANES - t_new, n_kv * hd), F32)], axis=0)

    z_parts, zn_parts = [], []
    for h in range(n_kv):
        qh = jnp.concatenate([q[:, (h * g + i) * hd:(h * g + i + 1) * hd] for i in range(g)], axis=0)
        kh = kbuf[slot, :, pl.ds(h, page, stride=n_kv), :].reshape(past, hd).astype(BF16)
        z_parts.append(lax.dot_general(qh, kh, (((1,), (1,)), ((), ())), preferred_element_type=F32))
        knh = kn[:, h * hd:(h + 1) * hd].astype(BF16)
        zn_parts.append(lax.dot_general(qh, knh, (((1,), (1,)), ((), ())), preferred_element_type=F32))
    z = jnp.concatenate(z_parts, axis=0)
    zn = jnp.concatenate(zn_parts, axis=0)

    r_io = lax.broadcasted_iota(jnp.int32, (rows, LANES), 0)
    c_io = lax.broadcasted_iota(jnp.int32, (rows, LANES), 1)
    t_row = r_io % t_new

    if not fox:
        valid_n = c_io < t_row
        cn = _cum_dot(jnp.where(valid_n, _softplus2(zn), 0.0), _tri_incl(LANES))
        wn = jnp.where(valid_n, jnp.exp2(zn - cn), 0.0)
        run = cn[:, 0:1]
        tk = CUMSUM_TILE
        tri = _tri_incl(tk)
        w_parts = [None] * (past // tk)
        for j in reversed(range(past // tk)):
            zb = z[:, j * tk:(j + 1) * tk]
            c = _cum_dot(_softplus2(zb), tri)
            w_parts[j] = jnp.exp2(zb - c - run)
            run = run + c[:, 0:1]
        w = jnp.concatenate(w_parts, axis=1)
        denom = None
    else:
        lfn = lfn_ref[0]
        nh = n_kv * g
        tri = _tri_strict(page)
        bias_n = jnp.dot(lfn, tri, preferred_element_type=F32, precision=lax.Precision.HIGHEST)
        carry = jnp.sum(lfn, axis=-1, keepdims=True)
        bias_parts = [None] * n_pages
        for p in reversed(range(n_pages)):
            lfp = lfbuf[slot, p]
            bias_parts[p] = jnp.dot(lfp, tri, preferred_element_type=F32,
                                    precision=lax.Precision.HIGHEST) + carry
            carry = carry + jnp.sum(lfp, axis=-1, keepdims=True)
        bias_h = jnp.concatenate(bias_parts, axis=1)
        bias = jnp.concatenate([jnp.broadcast_to(bias_h[i:i + 1], (t_new, past)) for i in range(nh)], axis=0)
        bias_new = jnp.concatenate([jnp.broadcast_to(bias_n[i:i + 1], (t_new, LANES)) for i in range(nh)], axis=0)
        s = z + bias * LOG2E
        sn = jnp.where(c_io <= t_row, zn + bias_new * LOG2E, NEG_BIG)
        m = jnp.maximum(jnp.max(s, axis=-1, keepdims=True), jnp.max(sn, axis=-1, keepdims=True))
        w = jnp.exp2(s - m)
        wn = jnp.exp2(sn - m)
        denom = jnp.sum(w, axis=-1, keepdims=True) + jnp.sum(wn, axis=-1, keepdims=True)

    wb = w.astype(BF16)
    wnb = wn.astype(BF16)
    o_cols = []
    rh = g * t_new
    for h in range(n_kv):
        vh = vbuf[slot, :, pl.ds(h, page, stride=n_kv), :].reshape(past, hd).astype(BF16)
        vnh = vn[:, h * hd:(h + 1) * hd].astype(BF16)
        oh = (jnp.dot(wb[h * rh:(h + 1) * rh], vh, preferred_element_type=F32)
              + jnp.dot(wnb[h * rh:(h + 1) * rh], vnh, preferred_element_type=F32))
        if fox:
            oh = oh / denom[h * rh:(h + 1) * rh]
        for i in range(g):
            o_cols.append(oh[i * t_new:(i + 1) * t_new])
    o_ref[0] = jnp.concatenate(o_cols, axis=1).astype(BF16)


def _sample_attention(page_table, q_b, k_new, v_new, k_pool, v_pool, lf_new_t, lf_pool_t, *, n_kv, g, hd):
    nb, t_new, dq = q_b.shape
    n_pages = page_table.shape[1]
    page = k_pool.shape[1] // n_kv
    nkv = n_kv * hd
    fox = lf_pool_t is not None
    in_specs = [pl.BlockSpec((1, t_new, dq), lambda b, pt: (b, 0, 0)),
                pl.BlockSpec((1, t_new, nkv), lambda b, pt: (b, 0, 0)),
                pl.BlockSpec((1, t_new, nkv), lambda b, pt: (b, 0, 0))]
    args = [q_b, k_new, v_new]
    scratch = [pltpu.VMEM((2, n_pages, page * n_kv, hd), F32),
               pltpu.VMEM((2, n_pages, page * n_kv, hd), F32)]
    if fox:
        nh = n_kv * g
        in_specs.append(pl.BlockSpec((1, nh, LANES), lambda b, pt: (b, 0, 0)))
        args.append(lf_new_t)
    in_specs += [pl.BlockSpec(memory_space=pl.ANY), pl.BlockSpec(memory_space=pl.ANY)]
    args += [k_pool, v_pool]
    if fox:
        in_specs.append(pl.BlockSpec(memory_space=pl.ANY))
        args.append(lf_pool_t)
        scratch.append(pltpu.VMEM((2, n_pages, nh, page), F32))
    scratch.append(pltpu.SemaphoreType.DMA((3, 2)))
    return pl.pallas_call(
        functools.partial(_sample_att_body, n_pages=n_pages, page=page, n_kv=n_kv, g=g, hd=hd,
                          t_new=t_new, fox=fox),
        grid_spec=pltpu.PrefetchScalarGridSpec(
            num_scalar_prefetch=1, grid=(nb,), in_specs=in_specs,
            out_specs=pl.BlockSpec((1, t_new, dq), lambda b, pt: (b, 0, 0)),
            scratch_shapes=scratch),
        out_shape=jax.ShapeDtypeStruct((nb, t_new, dq), BF16),
        compiler_params=_params("arbitrary"),
        name="fox_sample_attention" if fox else "sb_sample_attention",
    )(page_table, *args)


def _layernorm(u, gamma, beta):
    mu = jnp.mean(u, axis=-1, keepdims=True)
    uc = u - mu
    var = jnp.mean(uc * uc, axis=-1, keepdims=True)
    return uc * lax.rsqrt(var + LN_EPS) * gamma + beta


def _first_index_of_max(vals, mask, lane):
    top = jnp.max(jnp.where(mask, vals, NEG_BIG), axis=-1, keepdims=True)
    idx = jnp.min(jnp.where(mask & (vals == top), lane, 2 * LANES), axis=-1, keepdims=True)
    return top, idx


def _post_attn_body(op_ref, os_ref, xp_ref, xs_ref, g1_ref, sh2_ref, sc2_ref, w_ref, lng_ref, lnb_ref, rw_ref,
                    rb_ref, x1_ref, h2_ref, eid_ref, gate_ref, *, alpha, n_exp, n_prompt_tiles):
    is_prompt = pl.program_id(0) < n_prompt_tiles
    o = jnp.where(is_prompt, op_ref[...], os_ref[...])
    x = jnp.where(is_prompt, xp_ref[...], xs_ref[...])
    y = jnp.dot(o, w_ref[...], preferred_element_type=F32)
    x1 = _layernorm(alpha * x + g1_ref[0] * y, lng_ref[...], lnb_ref[...])
    x1_ref[...] = x1
    h2 = x1 * (1.0 + sc2_ref[0]) + sh2_ref[0]
    h2_ref[...] = h2

    logits = _dot_split(h2, rw_ref[...])
    lane = lax.broadcasted_iota(jnp.int32, logits.shape, 1)
    real = lane < n_exp
    logits = jnp.where(real, logits, NEG_BIG)
    e = jnp.exp(logits - jnp.max(logits, axis=-1, keepdims=True))
    probs = e / jnp.sum(e, axis=-1, keepdims=True)
    sel = probs + rb_ref[...]
    epg = n_exp // N_GROUPS
    best = first = second = None
    for grp in range(N_GROUPS):
        in_g = (lane >= grp * epg) & (lane < (grp + 1) * epg)
        top1, idx1 = _first_index_of_max(sel, in_g, lane)
        top2, idx2 = _first_index_of_max(sel, in_g & (lane != idx1), lane)
        score = top1 + top2
        if grp == 0:
            best, first, second = score, idx1, idx2
        else:
            better = score > best
            best = jnp.where(better, score, best)
            first = jnp.where(better, idx1, first)
            second = jnp.where(better, idx2, second)
    gate1 = jnp.sum(jnp.where(lane == first, probs, 0.0), axis=-1, keepdims=True)
    gate2 = jnp.sum(jnp.where(lane == second, probs, 0.0), axis=-1, keepdims=True)
    tot = gate1 + gate2
    eid_ref[...] = jnp.where(lane == 0, first, jnp.where(lane == 1, second, 0))
    gate_ref[...] = jnp.where(lane == 0, gate1 / tot, jnp.where(lane == 1, gate2 / tot, 0.0))


def _post_attention(o_p, o_s, x_p, x_s, mod, layer, w_out, ln_g, ln_b, rw_pad, rb_pad, *, alpha, n_exp):
    n_p, d = x_p.shape
    n = n_p + x_s.shape[0]
    tm = TOKEN_TILE
    n_prompt_tiles = n_p // tm
    kind = lambda i: i // n_prompt_tiles
    row = pl.BlockSpec((tm, d), lambda i: (i, 0))
    row_p = _prompt_rows(tm, d, n_prompt_tiles)
    row_s = _sample_rows(tm, d, n_prompt_tiles)
    modspec = lambda c: pl.BlockSpec((1, tm, d), lambda i: (layer, kind(i), c))
    const = lambda a: pl.BlockSpec(a.shape, lambda i: (0,) * a.ndim)
    lanes = pl.BlockSpec((tm, LANES), lambda i: (i, 0))
    return pl.pallas_call(
        functools.partial(_post_attn_body, alpha=alpha, n_exp=n_exp, n_prompt_tiles=n_prompt_tiles),
        grid=(n // tm,),
        in_specs=[row_p, row_s, row_p, row_s, modspec(2), modspec(3), modspec(4), const(w_out), const(ln_g),
                  const(ln_b), const(rw_pad), const(rb_pad)],
        out_specs=[row, row, lanes, lanes],
        out_shape=[jax.ShapeDtypeStruct((n, d), F32), jax.ShapeDtypeStruct((n, d), F32),
                   jax.ShapeDtypeStruct((n, LANES), jnp.int32), jax.ShapeDtypeStruct((n, LANES), F32)],
        compiler_params=_params("arbitrary"),
        name="post_attention",
    )(o_p, o_s, x_p, x_s, mod, mod, mod, w_out, ln_g, ln_b, rw_pad, rb_pad)


def _moe_body(texp_ref, trows_ref, src_ref, dst_ref, srcn_ref, gate_ref, wg_ref, wu_ref, wd_ref,
              h_hbm, out_hbm, xbuf, ybuf, sem, *, tm):
    t = pl.program_id(0)
    nt = pl.num_programs(0)
    slot = t % 2
    del texp_ref
    rows_of = lambda i: trows_ref[jnp.clip(i, 0, nt - 1)]

    def start_gather(idx_ref, sl):
        def chunk(c, _):
            for u in range(ROW_DMA_UNROLL):
                r = c * ROW_DMA_UNROLL + u
                pltpu.make_async_copy(h_hbm.at[idx_ref[0, 0, r]], xbuf.at[sl, r], sem.at[0, sl]).start()
            return 0
        lax.fori_loop(0, tm // ROW_DMA_UNROLL, chunk, 0)

    def wait_gather(sl):
        pltpu.make_async_copy(h_hbm.at[pl.ds(0, tm)], xbuf.at[sl], sem.at[0, sl]).wait()

    def start_scatter(sl, n_rows):
        def one(r):
            pltpu.make_async_copy(ybuf.at[sl, r], out_hbm.at[dst_ref[0, 0, r]], sem.at[1, sl]).start()

        def chunk(c, _):
            for u in range(ROW_DMA_UNROLL):
                one(c * ROW_DMA_UNROLL + u)
            return 0
        n_full = n_rows // ROW_DMA_UNROLL
        lax.fori_loop(0, n_full, chunk, 0)
        lax.fori_loop(n_full * ROW_DMA_UNROLL, n_rows, lambda r, _: (one(r), 0)[1], 0)

    def wait_scatter(sl, n_rows):
        p = tm
        while p >= 1:
            @pl.when((n_rows & p) != 0)
            def _(p=p):
                pltpu.make_async_copy(ybuf.at[sl, pl.ds(0, p)], out_hbm.at[pl.ds(0, p)], sem.at[1, sl]).wait()
            p //= 2

    @pl.when((t == 0) & (rows_of(0) > 0))
    def _():
        start_gather(src_ref, 0)

    @pl.when((t + 1 < nt) & (rows_of(t + 1) > 0))
    def _():
        start_gather(srcn_ref, 1 - slot)

    @pl.when(t >= 2)
    def _():
        wait_scatter(slot, rows_of(t - 2))

    @pl.when(rows_of(t) > 0)
    def _():
        wait_gather(slot)
        xb = xbuf[slot].astype(BF16)
        gp = jnp.dot(xb, wg_ref[0, 0].astype(BF16), preferred_element_type=F32)
        up = jnp.dot(xb, wu_ref[0, 0].astype(BF16), preferred_element_type=F32)
        act = gp / (1.0 + jnp.exp(-gp)) * up * gate_ref[...]
        ybuf[slot] = jnp.dot(act.astype(BF16), wd_ref[0, 0].astype(BF16), preferred_element_type=F32)
        start_scatter(slot, rows_of(t))

    @pl.when(t == nt - 1)
    def _():
        @pl.when(nt >= 2)
        def _():
            wait_scatter(1 - slot, rows_of(t - 1))
        wait_scatter(slot, rows_of(t))


def _moe(h2, tile_expert, tile_rows, row_src, row_dst, row_gate, w_gate, w_up, w_down, layer):
    n, d = h2.shape
    tm = MOE_ROW_TILE
    n_tiles = tile_expert.shape[0]
    f = w_gate.shape[-1]
    nxt = lambda t: jnp.minimum(t + 1, n_tiles - 1)
    smem_rows = lambda imap: pl.BlockSpec((1, 1, tm), imap, memory_space=pltpu.SMEM)
    return pl.pallas_call(
        functools.partial(_moe_body, tm=tm),
        grid_spec=pltpu.PrefetchScalarGridSpec(
            num_scalar_prefetch=2, grid=(n_tiles,),
            in_specs=[smem_rows(lambda t, te, tr: (t, 0, 0)),
                      smem_rows(lambda t, te, tr: (t, 0, 0)),
                      smem_rows(lambda t, te, tr: (nxt(t), 0, 0)),
                      pl.BlockSpec((tm, 1), lambda t, te, tr: (t, 0)),
                      pl.BlockSpec((1, 1, d, f), lambda t, te, tr: (layer, te[t], 0, 0)),
                      pl.BlockSpec((1, 1, d, f), lambda t, te, tr: (layer, te[t], 0, 0)),
                      pl.BlockSpec((1, 1, f, d), lambda t, te, tr: (layer, te[t], 0, 0)),
                      pl.BlockSpec(memory_space=pl.ANY)],
            out_specs=pl.BlockSpec(memory_space=pl.ANY),
            scratch_shapes=[pltpu.VMEM((2, tm, d), F32), pltpu.VMEM((2, tm, d), F32),
                            pltpu.SemaphoreType.DMA((2, 2))]),
        out_shape=jax.ShapeDtypeStruct((TOP_K * n, d), F32),
        compiler_params=_params("arbitrary"),
        name="sparse_experts",
    )(tile_expert, tile_rows, row_src, row_dst, row_src, row_gate, w_gate, w_up, w_down, h2)


def _routing_tables(eid, gate, n_exp, tm):
    n = eid.shape[0]
    n_assign = n * TOP_K
    n_tiles = (n_assign + n_exp * (tm - 1)) // tm + 1
    e_flat = eid.reshape(n_assign)
    order = jnp.argsort(e_flat, stable=True).astype(jnp.int32)
    counts = jnp.sum((e_flat[:, None] == jnp.arange(n_exp, dtype=jnp.int32)[None, :]).astype(jnp.int32), axis=0)
    starts = jnp.cumsum(counts) - counts
    tiles_per = (counts + tm - 1) // tm
    tile_starts = jnp.cumsum(tiles_per) - tiles_per
    n_used = jnp.sum(tiles_per).astype(jnp.int32)
    tile_ids = jnp.arange(n_tiles, dtype=jnp.int32)
    tile_ends = tile_starts + tiles_per
    tile_expert = jnp.sum((tile_ids[:, None] >= tile_ends[None, :]).astype(jnp.int32), axis=1)
    tile_expert = jnp.minimum(tile_expert, n_exp - 1)
    row_in_tile = jnp.arange(tm, dtype=jnp.int32)[None, :]
    within = (tile_ids - tile_starts[tile_expert])[:, None] * tm + row_in_tile
    valid = (within < counts[tile_expert][:, None]) & (tile_ids[:, None] < n_used)
    sorted_pos = jnp.clip(starts[tile_expert][:, None] + within, 0, n_assign - 1)
    assign = order[sorted_pos]
    tok = assign // TOP_K
    slot_k = assign % TOP_K
    row_src = jnp.where(valid, tok, 0).astype(jnp.int32)
    row_dst = jnp.where(valid, slot_k * n + tok, 0).astype(jnp.int32)
    row_gate = jnp.where(valid, gate.reshape(n_assign)[assign], 0.0).astype(F32)
    tile_rows = jnp.sum(valid.astype(jnp.int32), axis=1)
    return (tile_expert, tile_rows, row_src.reshape(n_tiles, 1, tm),
            row_dst.reshape(n_tiles, 1, tm), row_gate.reshape(n_tiles * tm, 1))


def _combine_body(x1_ref, ya_ref, yb_ref, g2_ref, lng_ref, lnb_ref, op_ref, os_ref, *, alpha, n_prompt_tiles,
                  nb, seq_major):
    i = pl.program_id(0)
    y = ya_ref[...] + yb_ref[...]
    out = _layernorm(alpha * x1_ref[...] + g2_ref[0] * y, lng_ref[...], lnb_ref[...])

    @pl.when(i < n_prompt_tiles)
    def _():
        op_ref[...] = out

    @pl.when(i >= n_prompt_tiles)
    def _():
        if seq_major:
            d = out.shape[1]
            for t in range(out.shape[0] // nb):
                os_ref[:, t * d:(t + 1) * d] = out[t * nb:(t + 1) * nb]
        else:
            os_ref[...] = out


def _combine(x1, y2, mod, layer, ln_g, ln_b, *, n_prompt_tiles, nb, alpha, seq_major):
    n, d = x1.shape
    tm = TOKEN_TILE
    n_p = n_prompt_tiles * tm
    n_s = n - n_p
    kind = lambda i: i // n_prompt_tiles
    row = pl.BlockSpec((tm, d), lambda i: (i, 0))
    const = lambda a: pl.BlockSpec(a.shape, lambda i: (0,) * a.ndim)
    if seq_major:
        s_spec = pl.BlockSpec((nb, (tm // nb) * d), lambda i: (0, jnp.maximum(i - n_prompt_tiles, 0)))
        s_shape = jax.ShapeDtypeStruct((nb, (n_s // nb) * d), F32)
    else:
        s_spec = _sample_rows(tm, d, n_prompt_tiles)
        s_shape = jax.ShapeDtypeStruct((n_s, d), F32)
    return pl.pallas_call(
        functools.partial(_combine_body, alpha=alpha, n_prompt_tiles=n_prompt_tiles, nb=nb, seq_major=seq_major),
        grid=(n // tm,),
        in_specs=[row, row, pl.BlockSpec((tm, d), lambda i: (n // tm + i, 0)),
                  pl.BlockSpec((1, tm, d), lambda i: (layer, kind(i), 5)), const(ln_g), const(ln_b)],
        out_specs=[_prompt_rows(tm, d, n_prompt_tiles), s_spec],
        out_shape=[jax.ShapeDtypeStruct((n_p, d), F32), s_shape],
        compiler_params=_params("arbitrary"),
        name="combine_layernorm",
    )(x1, y2, y2, mod, ln_g, ln_b)


def kernel(x_prompt, x_sample, cache_k_sb, cache_v_sb, cache_k_fox, cache_v_fox, cache_logf_fox, page_table,
           c_prompt, c_sample, ada_w, ada_b, ln_g, ln_b, w_in_sb, w_out_sb, w_in_fox, b_f_fox, w_out_fox,
           router_w, router_b, w_gate, w_up, w_down):
    batch, seq, d = x_prompt.shape
    nb, t_new, _ = x_sample.shape
    depth = ada_w.shape[0]
    n_pool, page, n_kv, hd = cache_k_sb.shape[1:]
    n_heads = d // hd
    g = n_heads // n_kv
    nq, nkv = n_heads * hd, n_kv * hd
    n_exp = router_w.shape[1]
    scale = float(hd) ** -0.5 * LOG2E
    alpha = (2.0 * depth) ** 0.25
    tm = TOKEN_TILE
    n_p, n_s = batch * seq, nb * t_new
    assert batch == 1 and n_p % tm == 0 and n_s % tm == 0 and tm % nb == 0
    assert seq % ATT_KV_TILE == 0 and seq % FOX_KV_TILE == 0 and seq % ATT_Q_TILE == 0
    assert ATT_KV_TILE % ATT_Q_TILE == 0 and FOX_KV_TILE % ATT_Q_TILE == 0 and ATT_KV_TILE % CUMSUM_TILE == 0
    assert (page * page_table.shape[1]) % CUMSUM_TILE == 0
    assert n_exp % N_GROUPS == 0 and n_exp <= LANES and n_heads <= LANES and t_new <= LANES
    n_prompt_tiles = n_p // tm

    x_p = x_prompt.reshape(n_p, d)
    x_s = jnp.transpose(x_sample, (1, 0, 2)).reshape(n_s, d)
    c_rows = jnp.concatenate([jnp.broadcast_to(c_prompt, (tm, d))] + [c_sample] * (tm // nb), axis=0)
    mod = _modulation(c_rows, ada_w, ada_b)

    rw_pad = _hi_lo_columns(router_w)
    rb_pad = jnp.pad(router_b.reshape(1, n_exp), ((0, 0), (0, LANES - n_exp)))

    def to_pos_major(a):
        return jnp.transpose(a, (1, 0, 2)).reshape(n_s, a.shape[-1])

    sb_out, fox_out = [], []
    for layer in range(depth):
        j = layer // 2
        fox = layer % 2 == 1
        if fox:
            w_in, w_out = w_in_fox[j], w_out_fox[j]
            w_main = w_in[:, :nq + 2 * nkv].astype(BF16)
            w_f = _hi_lo_columns(w_in[:, nq + 2 * nkv:])
            b_f = jnp.pad(b_f_fox[j].reshape(1, n_heads), ((0, 0), (0, LANES - n_heads)))
            k_pool, v_pool = cache_k_fox[j], cache_v_fox[j]
        else:
            w_in, w_out = w_in_sb[j], w_out_sb[j]
            w_main, w_f, b_f = w_in.astype(BF16), None, None
            k_pool, v_pool = cache_k_sb[j], cache_v_sb[j]
        res = _qkv_proj(x_p, x_s, mod, layer, w_main, w_f, b_f, nb=nb, nq=nq, nkv=nkv, hd=hd, scale=scale)
        q_p, k_p, v_p, kb_p, vb_p = res[:5]
        q_s = res[5].reshape(nb, t_new, nq)
        k_s, v_s = res[6].reshape(nb, t_new, nkv), res[7].reshape(nb, t_new, nkv)
        k_pool = k_pool.reshape(n_pool, page * n_kv, hd)
        v_pool = v_pool.reshape(n_pool, page * n_kv, hd)
        kv_out = (k_p.reshape(batch, seq, n_kv, hd), v_p.reshape(batch, seq, n_kv, hd),
                  k_s.reshape(nb, t_new, n_kv, hd), v_s.reshape(nb, t_new, n_kv, hd))
        if fox:
            lf_p = res[8][:, :n_heads]
            lf_s = res[9].reshape(nb, t_new, LANES)[:, :, :n_heads]
            ck_t = _cumsum_lanes(jnp.transpose(lf_p))
            lf_new_t = jnp.pad(jnp.transpose(lf_s, (0, 2, 1)), ((0, 0), (0, 0), (0, LANES - t_new)))
            lf_pool_t = jnp.transpose(cache_logf_fox[j], (0, 2, 1))
            o_p = _prompt_attention(q_p, kb_p, vb_p, ck_t, seq=seq, n_kv=n_kv, g=g, hd=hd)
            o_s = _sample_attention(page_table, q_s, k_s, v_s, k_pool, v_pool, lf_new_t, lf_pool_t,
                                    n_kv=n_kv, g=g, hd=hd)
            fox_out.append(kv_out[:2] + (lf_p.reshape(batch, seq, n_heads),) + kv_out[2:] + (lf_s,))
        else:
            o_p = _prompt_attention(q_p, kb_p, vb_p, None, seq=seq, n_kv=n_kv, g=g, hd=hd)
            o_s = _sample_attention(page_table, q_s, k_s, v_s, k_pool, v_pool, None, None,
                                    n_kv=n_kv, g=g, hd=hd)
            sb_out.append(kv_out)
        x1, h2, eid, gate = _post_attention(
            o_p, to_pos_major(o_s), x_p, x_s, mod, layer, w_out.astype(BF16),
            ln_g[layer, 0].reshape(1, d), ln_b[layer, 0].reshape(1, d), rw_pad, rb_pad, alpha=alpha, n_exp=n_exp)
        tables = _routing_tables(eid[:, :TOP_K], gate[:, :TOP_K], n_exp, MOE_ROW_TILE)
        y2 = _moe(h2, *tables, w_gate, w_up, w_down, layer)
        x_p, x_s = _combine(x1, y2, mod, layer, ln_g[layer, 1].reshape(1, d), ln_b[layer, 1].reshape(1, d),
                            n_prompt_tiles=n_prompt_tiles, nb=nb, alpha=alpha, seq_major=layer == depth - 1)

    y_prompt = x_p.reshape(batch, seq, d)
    y_sample = x_s.reshape(nb, t_new, d)
    stack = lambda items, i: jnp.stack([e[i] for e in items])
    return (y_prompt, y_sample,
            stack(sb_out, 0), stack(sb_out, 1), stack(sb_out, 2), stack(sb_out, 3),
            stack(fox_out, 0), stack(fox_out, 1), stack(fox_out, 2),
            stack(fox_out, 3), stack(fox_out, 4), stack(fox_out, 5))
```
